```python
import jax, jax.numpy as jnp
from jax import lax
import numpy as np

D_MODEL = 1024
BATCH = 8
SEQ = 2048
DEPTH = 1

D_PLE = 256
D_FF = 2816
MLSTM_HEADS = 4
MLSTM_HEAD_DIM = 128
MLSTM_WIDTH = MLSTM_HEADS * MLSTM_HEAD_DIM
GMLP_HEADS = 4
GMLP_HEAD_DIM = 128
GMLP_WIDTH = GMLP_HEADS * GMLP_HEAD_DIM
MIX_WIDTH = MLSTM_WIDTH + GMLP_WIDTH
CHUNK = 128
CONV_WIDTH = 4
N_NORMS = 8
EPS = 1e-6
IN_COLS = 4 * MLSTM_WIDTH + 2 * MLSTM_HEADS + 2 * GMLP_WIDTH

kernel_name = "hybrid_mlstm_gmlp_macaron_block"


def rms_norm(x, g):
    xf = x.astype(jnp.float32)
    y = xf * lax.rsqrt(jnp.mean(xf * xf, axis=-1, keepdims=True) + EPS)
    return (y * g.astype(jnp.float32)).astype(x.dtype)


def layer_norm(x, g, b):
    xf = x.astype(jnp.float32)
    mu = jnp.mean(xf, axis=-1, keepdims=True)
    var = jnp.mean(jnp.square(xf - mu), axis=-1, keepdims=True)
    y = (xf - mu) * lax.rsqrt(var + EPS)
    return (y * g.astype(jnp.float32) + b.astype(jnp.float32)).astype(x.dtype)


def swiglu(x, w_gu, w_down):
    g, u = jnp.split(x @ w_gu, 2, axis=-1)
    return (jax.nn.silu(g) * u) @ w_down


def causal_depthwise_conv(x, w, b):
    k_w, c = w.shape
    y = lax.conv_general_dilated(x, w[:, None, :], window_strides=(1,),
                                 padding=[(k_w - 1, 0)],
                                 dimension_numbers=('NWC', 'WIO', 'NWC'),
                                 feature_group_count=c)
    return y + b


def mlstm_chunkwise(q, k, v, log_i, log_f):
    bsz, nh, s, d = q.shape
    nc = s // CHUNK
    q = q.reshape(bsz, nh, nc, CHUNK, d)
    k = k.reshape(bsz, nh, nc, CHUNK, d)
    v = v.reshape(bsz, nh, nc, CHUNK, d)
    log_i = log_i.reshape(bsz, nh, nc, CHUNK)
    log_f = log_f.reshape(bsz, nh, nc, CHUNK)
    b = jnp.cumsum(log_f, axis=-1)
    a = b[..., -1]
    causal = jnp.tril(jnp.ones((CHUNK, CHUNK), dtype=bool))
    d_intra = jnp.where(causal, b[..., :, None] - b[..., None, :] + log_i[..., None, :], -jnp.inf)
    w_state = a[..., None] - b + log_i
    m_loc = jnp.max(w_state, axis=-1)
    e_state = jnp.exp(w_state - m_loc[..., None])
    ke = k * e_state[..., None]
    c_loc = jnp.einsum('bhcld,bhcle->bhcde', ke, v)
    n_loc = jnp.sum(ke, axis=3)

    def step(carry, xs):
        c_prev, n_prev, m_prev = carry
        a_c, m_l, c_l, n_l = xs
        m_new = jnp.maximum(a_c + m_prev, m_l)
        s_prev = jnp.exp(a_c + m_prev - m_new)
        s_loc = jnp.exp(m_l - m_new)
        c_new = s_prev[..., None, None] * c_prev + s_loc[..., None, None] * c_l
        n_new = s_prev[..., None] * n_prev + s_loc[..., None] * n_l
        return (c_new, n_new, m_new), (c_prev, n_prev, m_prev)

    init = (jnp.zeros((bsz, nh, d, d), jnp.float32),
            jnp.zeros((bsz, nh, d), jnp.float32),
            jnp.zeros((bsz, nh), jnp.float32))
    xs = (jnp.moveaxis(a, 2, 0), jnp.moveaxis(m_loc, 2, 0),
          jnp.moveaxis(c_loc, 2, 0), jnp.moveaxis(n_loc, 2, 0))
    _, (c_in, n_in, m_in) = lax.scan(step, init, xs)
    c_in = jnp.moveaxis(c_in, 0, 2)
    n_in = jnp.moveaxis(n_in, 0, 2)
    m_in = jnp.moveaxis(m_in, 0, 2)

    inter_log = b + m_in[..., None]
    m_t = jnp.maximum(inter_log, jnp.max(d_intra, axis=-1))
    e_inter = jnp.exp(inter_log - m_t)
    e_intra = jnp.exp(d_intra - m_t[..., None])
    qk = jnp.einsum('bhctd,bhcsd->bhcts', q, k) * e_intra
    num = (e_inter[..., None] * jnp.einsum('bhctd,bhcde->bhcte', q, c_in)
           + jnp.einsum('bhcts,bhcse->bhcte', qk, v))
    den = e_inter * jnp.einsum('bhctd,bhcd->bhct', q, n_in) + jnp.sum(qk, axis=-1)
    h = num / jnp.maximum(jnp.abs(den), jnp.exp(-m_t))[..., None]
    return h.reshape(bsz, nh, s, d)


def token_mixer(a, w_in, conv_w, conv_b, b_if, mh_norm_g, gmlp_ln_g, gmlp_ln_b,
                w_spatial, b_spatial, w_out):
    bsz, s, _ = a.shape
    mw, mh, md = MLSTM_WIDTH, MLSTM_HEADS, MLSTM_HEAD_DIM
    proj = a @ w_in
    qk_raw, v_m, o_pre, if_pre, u_g, v_g = jnp.split(
        proj, [2 * mw, 3 * mw, 4 * mw, 4 * mw + 2 * mh, 4 * mw + 2 * mh + GMLP_WIDTH], axis=-1)

    qk = jax.nn.silu(causal_depthwise_conv(qk_raw, conv_w, conv_b))
    q, k = jnp.split(qk, 2, axis=-1)

    def heads(t):
        return t.reshape(bsz, s, mh, md).transpose(0, 2, 1, 3).astype(jnp.float32)

    gates = (if_pre + b_if).astype(jnp.float32)
    log_i = gates[..., :mh].transpose(0, 2, 1)
    log_f = jax.nn.log_sigmoid(gates[..., mh:]).transpose(0, 2, 1)
    h = mlstm_chunkwise(heads(q), heads(k) * (md ** -0.5), heads(v_m), log_i, log_f)
    h = h.transpose(0, 2, 1, 3)
    h = h * lax.rsqrt(jnp.mean(h * h, axis=-1, keepdims=True) + EPS)
    h = h * mh_norm_g.reshape(mh, md).astype(jnp.float32)
    h_m = h.reshape(bsz, s, mw).astype(a.dtype) * jax.nn.sigmoid(o_pre)

    u_g = jax.nn.gelu(u_g, approximate=False)
    v_g = layer_norm(jax.nn.gelu(v_g, approximate=False), gmlp_ln_g, gmlp_ln_b)
    nc = s // CHUNK
    vc = v_g.reshape(bsz, nc, CHUNK, GMLP_HEADS, GMLP_HEAD_DIM)
    causal = jnp.tril(jnp.ones((CHUNK, CHUNK), dtype=bool))
    ws = jnp.where(causal, w_spatial, jnp.zeros_like(w_spatial))
    sv = jnp.einsum('gts,bcsge->bctge', ws, vc) + b_spatial.T[:, :, None]
    h_g = u_g * sv.reshape(bsz, s, GMLP_WIDTH)

    return jnp.concatenate([h_m, h_g], axis=-1) @ w_out


def setup_inputs(seed: int = 0) -> dict:
    key = jax.random.key(seed)
    ks = jax.random.split(key, 24)
    f32 = jnp.float32
    nrm = lambda k, shape, scale: (jax.random.normal(k, shape, f32) * scale)
    L = DEPTH
    x = nrm(ks[0], (BATCH, SEQ, D_MODEL), 1.0)
    p = nrm(ks[1], (L, BATCH, SEQ, D_PLE), 1.0)
    ffn1_gu = nrm(ks[2], (L, D_MODEL, 2 * D_FF), D_MODEL ** -0.5)
    ffn1_down = nrm(ks[3], (L, D_FF, D_MODEL), D_FF ** -0.5)
    ffn2_gu = nrm(ks[4], (L, D_MODEL, 2 * D_FF), D_MODEL ** -0.5)
    ffn2_down = nrm(ks[5], (L, D_FF, D_MODEL), D_FF ** -0.5)
    w_in = nrm(ks[6], (L, D_MODEL, IN_COLS), D_MODEL ** -0.5)
    conv_w = nrm(ks[7], (L, CONV_WIDTH, 2 * MLSTM_WIDTH), CONV_WIDTH ** -0.5)
    conv_b = nrm(ks[8], (L, 2 * MLSTM_WIDTH), 0.02)
    b_i = nrm(ks[9], (L, MLSTM_HEADS), 0.1)
    b_f = 3.0 + nrm(ks[10], (L, MLSTM_HEADS), 0.5)
    b_if = jnp.concatenate([b_i, b_f], axis=-1)
    mh_norm_g = 1.0 + nrm(ks[11], (L, MLSTM_WIDTH), 0.02)
    gmlp_ln_g = 1.0 + nrm(ks[12], (L, GMLP_WIDTH), 0.02)
    gmlp_ln_b = nrm(ks[13], (L, GMLP_WIDTH), 0.02)
    w_spatial = nrm(ks[14], (L, GMLP_HEADS, CHUNK, CHUNK), CHUNK ** -0.5)
    b_spatial = 1.0 + nrm(ks[15], (L, GMLP_HEADS, CHUNK), 0.1)
    w_out = nrm(ks[16], (L, MIX_WIDTH, D_MODEL), MIX_WIDTH ** -0.5)
    w_ple = nrm(ks[17], (L, D_PLE, D_MODEL), D_PLE ** -0.5)
    w_ple_gate = nrm(ks[18], (L, D_MODEL, D_MODEL), D_MODEL ** -0.5)
    norm_g = 1.0 + nrm(ks[19], (L, N_NORMS, D_MODEL), 0.02)
    return {"x": x, "p": p, "ffn1_gu": ffn1_gu, "ffn1_down": ffn1_down,
            "ffn2_gu": ffn2_gu, "ffn2_down": ffn2_down, "w_in": w_in,
            "conv_w": conv_w, "conv_b": conv_b, "b_if": b_if,
            "mh_norm_g": mh_norm_g, "gmlp_ln_g": gmlp_ln_g, "gmlp_ln_b": gmlp_ln_b,
            "w_spatial": w_spatial, "b_spatial": b_spatial, "w_out": w_out,
            "w_ple": w_ple, "w_ple_gate": w_ple_gate, "norm_g": norm_g}


def reference(x, p, ffn1_gu, ffn1_down, ffn2_gu, ffn2_down, w_in, conv_w, conv_b,
              b_if, mh_norm_g, gmlp_ln_g, gmlp_ln_b, w_spatial, b_spatial, w_out,
              w_ple, w_ple_gate, norm_g):
    h = x
    for i in range(DEPTH):
        g = norm_g[i]
        h = h + 0.5 * rms_norm(swiglu(rms_norm(h, g[0]), ffn1_gu[i], ffn1_down[i]), g[1])
        mix = token_mixer(rms_norm(h, g[2]), w_in[i], conv_w[i], conv_b[i], b_if[i],
                          mh_norm_g[i], gmlp_ln_g[i], gmlp_ln_b[i], w_spatial[i],
                          b_spatial[i], w_out[i])
        h = h + rms_norm(mix, g[3])
        h = h + 0.5 * rms_norm(swiglu(rms_norm(h, g[4]), ffn2_gu[i], ffn2_down[i]), g[5])
        gate = jax.nn.sigmoid(rms_norm(h, g[6]) @ w_ple_gate[i])
        h = h + rms_norm(gate * (p[i] @ w_ple[i]), g[7])
    return h
```

```python
import functools

import jax
import jax.numpy as jnp
import numpy as np
from jax import lax
from jax.experimental import pallas as pl
from jax.experimental.pallas import tpu as pltpu

D_MODEL = 1024
D_PLE = 256
D_FF = 2816
HEADS = 4
HEAD_DIM = 128
MLSTM_WIDTH = HEADS * HEAD_DIM
GMLP_WIDTH = HEADS * HEAD_DIM
CHUNK = 128
CONV_WIDTH = 4
EPS = 1e-6

LANES = 128
SUBLANES = 8

FFN_ROWS = 512
FFN_FC = 256
MIX_ROWS = 512
PLE_ROWS = 1024
VMEM_LIMIT = 56 * 1024 * 1024

F32 = jnp.float32
BF16 = jnp.bfloat16


def _rms(x, g):
    return x * lax.rsqrt(jnp.mean(x * x, axis=-1, keepdims=True) + EPS) * g


def _sigmoid(x):
    return 1.0 / (1.0 + jnp.exp(-x))


def _gelu_exact(x):
    return 0.5 * x * (1.0 + lax.erf(x * np.float32(np.sqrt(0.5))))


def _log_sigmoid(x):
    return -(jnp.maximum(-x, 0.0) + jnp.log1p(jnp.exp(-jnp.abs(x))))


def _dot(a, b):
    return jnp.dot(a, b, preferred_element_type=F32)


def _const_spec(shape):
    nd = len(shape)
    return pl.BlockSpec(shape, lambda *_: (0,) * nd, pipeline_mode=pl.Buffered(1))


def _ffn_kernel(x_ref, gin_ref, gout_ref, gu_ref, dn_ref, o_ref, a_scr, acc_scr):
    n_f = gu_ref.shape[0]
    fc = dn_ref.shape[1]
    a_scr[...] = _rms(x_ref[...], gin_ref[...]).astype(BF16)
    acc_scr[...] = jnp.zeros_like(acc_scr)

    def body(f, carry):
        gu = _dot(a_scr[...], gu_ref[f])
        g = gu[:, :fc]
        u = gu[:, fc:]
        act = (g * _sigmoid(g)) * u
        acc_scr[...] += _dot(act.astype(BF16), dn_ref[f])
        return carry

    lax.fori_loop(0, n_f, body, 0)
    o_ref[...] = x_ref[...] + 0.5 * _rms(acc_scr[...], gout_ref[...])


def _ffn(h, g_in, g_out, w_gu, w_down):
    m = h.shape[0]
    n_f = D_FF // FFN_FC
    gate = w_gu[:, :D_FF].reshape(D_MODEL, n_f, FFN_FC)
    up = w_gu[:, D_FF:].reshape(D_MODEL, n_f, FFN_FC)
    gu = jnp.concatenate([gate, up], axis=2).transpose(1, 0, 2).astype(BF16)
    dn = w_down.reshape(n_f, FFN_FC, D_MODEL).astype(BF16)
    row_spec = pl.BlockSpec((FFN_ROWS, D_MODEL), lambda i: (i, 0))
    return pl.pallas_call(
        _ffn_kernel,
        grid=(m // FFN_ROWS,),
        in_specs=[row_spec, _const_spec((1, D_MODEL)), _const_spec((1, D_MODEL)),
                  _const_spec(gu.shape), _const_spec(dn.shape)],
        out_specs=row_spec,
        out_shape=jax.ShapeDtypeStruct(h.shape, F32),
        scratch_shapes=[pltpu.VMEM((FFN_ROWS, D_MODEL), BF16),
                        pltpu.VMEM((FFN_ROWS, D_MODEL), F32)],
        compiler_params=pltpu.CompilerParams(
            dimension_semantics=("arbitrary",), vmem_limit_bytes=VMEM_LIMIT),
        name="ffn",
    )(h, g_in.reshape(1, D_MODEL), g_out.reshape(1, D_MODEL), gu, dn)


def _cumsum_rows(tril_bf16, x):
    hi = x.astype(BF16)
    r1 = x - hi.astype(F32)
    mid = r1.astype(BF16)
    lo = (r1 - mid.astype(F32)).astype(BF16)
    return _dot(tril_bf16, hi) + _dot(tril_bf16, mid) + _dot(tril_bf16, lo)


def _mixer_kernel(h_ref, g2_ref, g3_ref, wmain_ref, wif_ref, bi_ref, bf_ref,
                  convw_ref, convb_ref, mhg_ref, lng_ref, lnb_ref, ws_ref, bs_ref,
                  wout_ref, o_ref,
                  tail_scr, c_scr, n_scr, m_scr,
                  q_scr, k_scr, kf_scr, v_scr, gi_scr, lf_scr, hm_scr, vn_scr, ug_scr,
                  mix_scr):
    rows = h_ref.shape[0]
    mw = MLSTM_WIDTH

    @pl.when(pl.program_id(1) == 0)
    def _():
        tail_scr[...] = jnp.zeros_like(tail_scr)
        c_scr[...] = jnp.zeros_like(c_scr)
        n_scr[...] = jnp.zeros_like(n_scr)
        m_scr[...] = jnp.zeros_like(m_scr)

    a = _rms(h_ref[...], g2_ref[...]).astype(BF16)

    qk_raw = _dot(a, wmain_ref[:, 0:2 * mw])
    tail = tail_scr[...]
    row8 = lax.broadcasted_iota(jnp.int32, (SUBLANES, 2 * mw), 0)
    conv = qk_raw * convw_ref[CONV_WIDTH - 1:CONV_WIDTH, :] + convb_ref[...]
    for k in range(1, CONV_WIDTH):
        xr = pltpu.roll(qk_raw, k, axis=0)
        tr = pltpu.roll(tail, k, axis=0)
        first = jnp.where(row8 < k, tr, xr[0:SUBLANES])
        xs = jnp.concatenate([first, xr[SUBLANES:]], axis=0)
        conv = conv + xs * convw_ref[CONV_WIDTH - 1 - k:CONV_WIDTH - k, :]
    tail_scr[...] = qk_raw[rows - SUBLANES:rows]
    qk_act = conv * _sigmoid(conv)
    q_scr[...] = qk_act[:, 0:mw].astype(BF16)
    k_scaled = qk_act[:, mw:2 * mw] * np.float32(HEAD_DIM ** -0.5)
    kf_scr[...] = k_scaled
    k_scr[...] = k_scaled.astype(BF16)

    v_scr[...] = _dot(a, wmain_ref[:, 2 * mw:3 * mw]).astype(BF16)
    o_pre = _dot(a, wmain_ref[:, 3 * mw:4 * mw])

    gif = _dot(a, wif_ref[...])
    gi_scr[...] = gif[:, 0:LANES] + bi_ref[...]
    lf_scr[...] = _log_sigmoid(gif[:, LANES:2 * LANES] + bf_ref[...])

    ug_scr[...] = _gelu_exact(_dot(a, wmain_ref[:, 4 * mw:4 * mw + GMLP_WIDTH]))
    vg = _gelu_exact(_dot(a, wmain_ref[:, 4 * mw + GMLP_WIDTH:4 * mw + 2 * GMLP_WIDTH]))
    mu = jnp.mean(vg, axis=-1, keepdims=True)
    var = jnp.mean(jnp.square(vg - mu), axis=-1, keepdims=True)
    vn = (vg - mu) * lax.rsqrt(var + EPS) * lng_ref[...] + lnb_ref[...]
    vn_scr[...] = vn.astype(BF16)

    r_i = lax.broadcasted_iota(jnp.int32, (CHUNK, CHUNK), 0)
    c_i = lax.broadcasted_iota(jnp.int32, (CHUNK, CHUNK), 1)
    causal = c_i <= r_i
    tril_bf16 = jnp.where(causal, 1.0, 0.0).astype(BF16)
    neg_inf = np.float32(-np.inf)

    def chunk_body(c, carry):
        r0 = pl.multiple_of(c * CHUNK, CHUNK)
        rs = pl.ds(r0, CHUNK)
        logi = gi_scr[rs, :]
        b = _cumsum_rows(tril_bf16, lf_scr[rs, :])
        logi_t = logi.T
        b_t = b.T
        for hd in range(HEADS):
            cs = slice(hd * HEAD_DIM, (hd + 1) * HEAD_DIM)
            bcol = b[:, hd:hd + 1]
            icol = logi[:, hd:hd + 1]
            brow = b_t[hd:hd + 1, :]
            irow = logi_t[hd:hd + 1, :]
            a_h = b[CHUNK - 1:CHUNK, hd:hd + 1]
            m_in = m_scr[hd][:, 0:1]
            c_in = c_scr[hd]
            n_in = n_scr[hd]

            d = jnp.where(causal, bcol - brow + irow, neg_inf)
            m_intra = jnp.max(d, axis=-1, keepdims=True)
            inter = bcol + m_in
            m_t = jnp.maximum(inter, m_intra)
            e_inter = jnp.exp(inter - m_t)
            e_intra = jnp.exp(d - m_t)

            qh = q_scr[rs, cs]
            kh = k_scr[rs, cs]
            vh = v_scr[rs, cs]
            s = lax.dot_general(qh, kh, (((1,), (1,)), ((), ())),
                                preferred_element_type=F32)
            qk = s * e_intra
            num = (e_inter * _dot(qh, c_in.astype(BF16))
                   + _dot(qk.astype(BF16), vh))
            den = (e_inter * jnp.sum(qh.astype(F32) * n_in, axis=-1, keepdims=True)
                   + jnp.sum(qk, axis=-1, keepdims=True))
            hh = num / jnp.maximum(jnp.abs(den), jnp.exp(-m_t))
            hh = hh * lax.rsqrt(jnp.mean(hh * hh, axis=-1, keepdims=True) + EPS)
            hm_scr[rs, cs] = hh * mhg_ref[:, cs]

            wst = a_h - bcol + icol
            m_loc = jnp.max(wst, axis=0, keepdims=True)
            ke = kf_scr[rs, cs] * jnp.exp(wst - m_loc)
            c_loc = _dot(ke.T.astype(BF16), vh)
            n_loc = jnp.sum(ke, axis=0, keepdims=True)
            m_new = jnp.maximum(a_h + m_in, m_loc)
            s_prev = jnp.exp(a_h + m_in - m_new)
            s_loc = jnp.exp(m_loc - m_new)
            c_scr[hd] = s_prev * c_in + s_loc * c_loc
            n_scr[hd] = s_prev * n_in + s_loc * n_loc
            m_scr[hd] = jnp.broadcast_to(m_new, (1, LANES))

            w_sp = jnp.where(causal, ws_ref[hd], 0.0).astype(BF16)
            sv = _dot(w_sp, vn_scr[rs, cs]) + bs_ref[:, hd:hd + 1]
            mix_scr[rs, mw + hd * HEAD_DIM:mw + (hd + 1) * HEAD_DIM] = (
                ug_scr[rs, cs] * sv).astype(BF16)
        return carry

    lax.fori_loop(0, rows // CHUNK, chunk_body, 0)

    mix_scr[:, 0:mw] = (hm_scr[...] * _sigmoid(o_pre)).astype(BF16)
    mix = _dot(mix_scr[...], wout_ref[...])
    o_ref[...] = h_ref[...] + _rms(mix, g3_ref[...])


def _mixer(h, batch, seq, g2, g3, w_in, conv_w, conv_b, b_if, mh_norm_g,
           gmlp_ln_g, gmlp_ln_b, w_spatial, b_spatial, w_out):
    mw = MLSTM_WIDTH
    n_j = seq // MIX_ROWS
    gate0 = 4 * mw
    w_main = jnp.concatenate(
        [w_in[:, :gate0], w_in[:, gate0 + 2 * HEADS:]], axis=1).astype(BF16)
    w_if = jnp.zeros((D_MODEL, 2 * LANES), F32)
    w_if = w_if.at[:, 0:HEADS].set(w_in[:, gate0:gate0 + HEADS])
    w_if = w_if.at[:, LANES:LANES + HEADS].set(w_in[:, gate0 + HEADS:gate0 + 2 * HEADS])
    w_if = w_if.astype(BF16)
    b_i = jnp.zeros((1, LANES), F32).at[0, 0:HEADS].set(b_if[:HEADS])
    b_f = jnp.zeros((1, LANES), F32).at[0, 0:HEADS].set(b_if[HEADS:])
    row_spec = pl.BlockSpec((MIX_ROWS, D_MODEL), lambda b, j: (b * n_j + j, 0))
    args = (h, g2.reshape(1, D_MODEL), g3.reshape(1, D_MODEL), w_main, w_if, b_i, b_f,
            conv_w, conv_b.reshape(1, 2 * mw), mh_norm_g.reshape(1, mw),
            gmlp_ln_g.reshape(1, GMLP_WIDTH), gmlp_ln_b.reshape(1, GMLP_WIDTH),
            w_spatial, b_spatial.T, w_out.astype(BF16))
    in_specs = [row_spec] + [_const_spec(x.shape) for x in args[1:]]
    return pl.pallas_call(
        _mixer_kernel,
        grid=(batch, n_j),
        in_specs=in_specs,
        out_specs=row_spec,
        out_shape=jax.ShapeDtypeStruct(h.shape, F32),
        scratch_shapes=[
            pltpu.VMEM((SUBLANES, 2 * mw), F32),
            pltpu.VMEM((HEADS, HEAD_DIM, HEAD_DIM), F32),
            pltpu.VMEM((HEADS, 1, HEAD_DIM), F32),
            pltpu.VMEM((HEADS, 1, LANES), F32),
            pltpu.VMEM((MIX_ROWS, mw), BF16),
            pltpu.VMEM((MIX_ROWS, mw), BF16),
            pltpu.VMEM((MIX_ROWS, mw), F32),
            pltpu.VMEM((MIX_ROWS, mw), BF16),
            pltpu.VMEM((MIX_ROWS, LANES), F32),
            pltpu.VMEM((MIX_ROWS, LANES), F32),
            pltpu.VMEM((MIX_ROWS, mw), F32),
            pltpu.VMEM((MIX_ROWS, GMLP_WIDTH), BF16),
            pltpu.VMEM((MIX_ROWS, GMLP_WIDTH), F32),
            pltpu.VMEM((MIX_ROWS, mw + GMLP_WIDTH), BF16),
        ],
        compiler_params=pltpu.CompilerParams(
            dimension_semantics=("arbitrary", "arbitrary"),
            vmem_limit_bytes=VMEM_LIMIT),
        name="mixer",
    )(*args)


def _ple_kernel(h_ref, p_ref, g6_ref, g7_ref, wg_ref, wp_ref, o_ref):
    x = h_ref[...]
    gate = _sigmoid(_dot(_rms(x, g6_ref[...]).astype(BF16), wg_ref[...]))
    emb = _dot(p_ref[...].astype(BF16), wp_ref[...])
    o_ref[...] = x + _rms(gate * emb, g7_ref[...])


def _ple(h, p, g6, g7, w_gate, w_ple):
    m = h.shape[0]
    row_spec = pl.BlockSpec((PLE_ROWS, D_MODEL), lambda i: (i, 0))
    p_spec = pl.BlockSpec((PLE_ROWS, D_PLE), lambda i: (i, 0))
    return pl.pallas_call(
        _ple_kernel,
        grid=(m // PLE_ROWS,),
        in_specs=[row_spec, p_spec, _const_spec((1, D_MODEL)), _const_spec((1, D_MODEL)),
                  _const_spec(w_gate.shape), _const_spec(w_ple.shape)],
        out_specs=row_spec,
        out_shape=jax.ShapeDtypeStruct(h.shape, F32),
        compiler_params=pltpu.CompilerParams(
            dimension_semantics=("arbitrary",), vmem_limit_bytes=VMEM_LIMIT),
        name="ple",
    )(h, p, g6.reshape(1, D_MODEL), g7.reshape(1, D_MODEL),
      w_gate.astype(BF16), w_ple.astype(BF16))


def kernel(x, p, ffn1_gu, ffn1_down, ffn2_gu, ffn2_down, w_in, conv_w, conv_b, b_if,
           mh_norm_g, gmlp_ln_g, gmlp_ln_b, w_spatial, b_spatial, w_out, w_ple,
           w_ple_gate, norm_g):
    batch, seq, _ = x.shape
    depth = p.shape[0]
    h = x.reshape(batch * seq, D_MODEL)
    for i in range(depth):
        g = norm_g[i]
        h = _ffn(h, g[0], g[1], ffn1_gu[i], ffn1_down[i])
        h = _mixer(h, batch, seq, g[2], g[3], w_in[i], conv_w[i], conv_b[i], b_if[i],
                   mh_norm_g[i], gmlp_ln_g[i], gmlp_ln_b[i], w_spatial[i], b_spatial[i],
                   w_out[i])
        h = _ffn(h, g[4], g[5], ffn2_gu[i], ffn2_down[i])
        h = _ple(h, p[i].reshape(batch * seq, D_PLE), g[6], g[7], w_ple_gate[i], w_ple[i])
    return h.reshape(batch, seq, D_MODEL)
```

```python
import functools

import jax
import jax.numpy as jnp
import numpy as np
from jax import lax
from jax.experimental import pallas as pl
from jax.experimental.pallas import tpu as pltpu

D_MODEL = 1024
D_PLE = 256
D_FF = 2816
HEADS = 4
HEAD_DIM = 128
MLSTM_WIDTH = HEADS * HEAD_DIM
GMLP_WIDTH = HEADS * HEAD_DIM
CHUNK = 128
CONV_WIDTH = 4
EPS = 1e-6

LANES = 128
SUBLANES = 8

FFN_ROWS = 512
FFN_FC = 256
MIX_ROWS = 512
PLE_ROWS = 1024
VMEM_LIMIT = 56 * 1024 * 1024

F32 = jnp.float32
BF16 = jnp.bfloat16


def _rms(x, g):
    return x * lax.rsqrt(jnp.mean(x * x, axis=-1, keepdims=True) + EPS) * g


def _sigmoid(x):
    return 1.0 / (1.0 + jnp.exp(-x))


def _gelu_exact(x):
    return 0.5 * x * (1.0 + lax.erf(x * np.float32(np.sqrt(0.5))))


def _log_sigmoid(x):
    return -(jnp.maximum(-x, 0.0) + jnp.log1p(jnp.exp(-jnp.abs(x))))


def _dot(a, b):
    return jnp.dot(a, b, preferred_element_type=F32)


def _const_spec(shape):
    nd = len(shape)
    return pl.BlockSpec(shape, lambda *_: (0,) * nd, pipeline_mode=pl.Buffered(1))


def _ffn_kernel(x_ref, gin_ref, gout_ref, gu_ref, dn_ref, o_ref,
                a_scr, gua_scr, gub_scr, acc_scr):
    n_f = dn_ref.shape[0]
    fc = dn_ref.shape[1]
    a_scr[...] = _rms(x_ref[...], gin_ref[...]).astype(BF16)
    acc_scr[...] = jnp.zeros_like(acc_scr)

    def project(f, gu_scr):
        gu_scr[:, 0:fc] = _dot(a_scr[...], gu_ref[:, f * fc:(f + 1) * fc])
        gu_scr[:, fc:2 * fc] = _dot(a_scr[...], gu_ref[:, D_FF + f * fc:D_FF + (f + 1) * fc])

    def consume(f, gu_scr):
        g = gu_scr[:, 0:fc]
        u = gu_scr[:, fc:2 * fc]
        act = (g * _sigmoid(g)) * u
        acc_scr[...] += _dot(act.astype(BF16), dn_ref[f])

    bufs = (gua_scr, gub_scr)
    project(0, bufs[0])
    for f in range(n_f):
        if f + 1 < n_f:
            project(f + 1, bufs[(f + 1) % 2])
        consume(f, bufs[f % 2])
    o_ref[...] = x_ref[...] + 0.5 * _rms(acc_scr[...], gout_ref[...])


def _ffn(h, g_in, g_out, w_gu, w_down):
    m = h.shape[0]
    n_f = D_FF // FFN_FC
    gu = w_gu.astype(BF16)
    dn = w_down.reshape(n_f, FFN_FC, D_MODEL).astype(BF16)
    row_spec = pl.BlockSpec((FFN_ROWS, D_MODEL), lambda i: (i, 0))
    return pl.pallas_call(
        _ffn_kernel,
        grid=(m // FFN_ROWS,),
        in_specs=[row_spec, _const_spec((1, D_MODEL)), _const_spec((1, D_MODEL)),
                  _const_spec(gu.shape), _const_spec(dn.shape)],
        out_specs=row_spec,
        out_shape=jax.ShapeDtypeStruct(h.shape, F32),
        scratch_shapes=[pltpu.VMEM((FFN_ROWS, D_MODEL), BF16),
                        pltpu.VMEM((FFN_ROWS, 2 * FFN_FC), F32),
                        pltpu.VMEM((FFN_ROWS, 2 * FFN_FC), F32),
                        pltpu.VMEM((FFN_ROWS, D_MODEL), F32)],
        compiler_params=pltpu.CompilerParams(
            dimension_semantics=("arbitrary",), vmem_limit_bytes=VMEM_LIMIT),
        name="ffn",
    )(h, g_in.reshape(1, D_MODEL), g_out.reshape(1, D_MODEL), gu, dn)


def _cumsum_rows(tril_bf16, x):
    hi = x.astype(BF16)
    r1 = x - hi.astype(F32)
    mid = r1.astype(BF16)
    lo = (r1 - mid.astype(F32)).astype(BF16)
    return _dot(tril_bf16, hi) + _dot(tril_bf16, mid) + _dot(tril_bf16, lo)


def _mixer_kernel(h_ref, g2_ref, g3_ref, wmain_ref, wif_ref, bi_ref, bf_ref,
                  convw_ref, convb_ref, mhg_ref, lng_ref, lnb_ref, ws_ref, bs_ref,
                  wout_ref, o_ref,
                  tail_scr, c_scr, n_scr, m_scr,
                  q_scr, k_scr, kf_scr, v_scr, gi_scr, lf_scr, hm_scr, vn_scr, ug_scr,
                  mix_scr):
    rows = h_ref.shape[0]
    mw = MLSTM_WIDTH

    @pl.when(pl.program_id(1) == 0)
    def _():
        tail_scr[...] = jnp.zeros_like(tail_scr)
        c_scr[...] = jnp.zeros_like(c_scr)
        n_scr[...] = jnp.zeros_like(n_scr)
        m_scr[...] = jnp.zeros_like(m_scr)

    a = _rms(h_ref[...], g2_ref[...]).astype(BF16)

    qk_raw = _dot(a, wmain_ref[:, 0:2 * mw])
    tail = tail_scr[...]
    row8 = lax.broadcasted_iota(jnp.int32, (SUBLANES, 2 * mw), 0)
    conv = qk_raw * convw_ref[CONV_WIDTH - 1:CONV_WIDTH, :] + convb_ref[...]
    for k in range(1, CONV_WIDTH):
        xr = pltpu.roll(qk_raw, k, axis=0)
        tr = pltpu.roll(tail, k, axis=0)
        first = jnp.where(row8 < k, tr, xr[0:SUBLANES])
        xs = jnp.concatenate([first, xr[SUBLANES:]], axis=0)
        conv = conv + xs * convw_ref[CONV_WIDTH - 1 - k:CONV_WIDTH - k, :]
    tail_scr[...] = qk_raw[rows - SUBLANES:rows]
    qk_act = conv * _sigmoid(conv)
    q_scr[...] = qk_act[:, 0:mw].astype(BF16)
    k_scaled = qk_act[:, mw:2 * mw] * np.float32(HEAD_DIM ** -0.5)
    kf_scr[...] = k_scaled
    k_scr[...] = k_scaled.astype(BF16)

    v_scr[...] = _dot(a, wmain_ref[:, 2 * mw:3 * mw]).astype(BF16)
    o_pre = _dot(a, wmain_ref[:, 3 * mw:4 * mw])

    gif = _dot(a, wif_ref[...])
    gi_scr[...] = gif[:, 0:LANES] + bi_ref[...]
    lf_scr[...] = _log_sigmoid(gif[:, LANES:2 * LANES] + bf_ref[...])

    ug_scr[...] = _gelu_exact(_dot(a, wmain_ref[:, 4 * mw:4 * mw + GMLP_WIDTH]))
    vg = _gelu_exact(_dot(a, wmain_ref[:, 4 * mw + GMLP_WIDTH:4 * mw + 2 * GMLP_WIDTH]))
    mu = jnp.mean(vg, axis=-1, keepdims=True)
    var = jnp.mean(jnp.square(vg - mu), axis=-1, keepdims=True)
    vn = (vg - mu) * lax.rsqrt(var + EPS) * lng_ref[...] + lnb_ref[...]
    vn_scr[...] = vn.astype(BF16)

    r_i = lax.broadcasted_iota(jnp.int32, (CHUNK, CHUNK), 0)
    c_i = lax.broadcasted_iota(jnp.int32, (CHUNK, CHUNK), 1)
    causal = c_i <= r_i
    tril_bf16 = jnp.where(causal, 1.0, 0.0).astype(BF16)
    neg_inf = np.float32(-np.inf)

    def chunk_body(c, carry):
        r0 = pl.multiple_of(c * CHUNK, CHUNK)
        rs = pl.ds(r0, CHUNK)
        logi = gi_scr[rs, :]
        b = _cumsum_rows(tril_bf16, lf_scr[rs, :])
        logi_t = logi.T
        b_t = b.T
        for hd in range(HEADS):
            cs = slice(hd * HEAD_DIM, (hd + 1) * HEAD_DIM)
            bcol = b[:, hd:hd + 1]
            icol = logi[:, hd:hd + 1]
            brow = b_t[hd:hd + 1, :]
            irow = logi_t[hd:hd + 1, :]
            a_h = b[CHUNK - 1:CHUNK, hd:hd + 1]
            m_in = m_scr[hd][:, 0:1]
            c_in = c_scr[hd]
            n_in = n_scr[hd]

            d = jnp.where(causal, bcol - brow + irow, neg_inf)
            m_intra = jnp.max(d, axis=-1, keepdims=True)
            inter = bcol + m_in
            m_t = jnp.maximum(inter, m_intra)
            e_inter = jnp.exp(inter - m_t)
            e_intra = jnp.exp(d - m_t)

            qh = q_scr[rs, cs]
            kh = k_scr[rs, cs]
            vh = v_scr[rs, cs]
            s = lax.dot_general(qh, kh, (((1,), (1,)), ((), ())),
                                preferred_element_type=F32)
            qk = s * e_intra
            num = (e_inter * _dot(qh, c_in.astype(BF16))
                   + _dot(qk.astype(BF16), vh))
            den = (e_inter * jnp.sum(qh.astype(F32) * n_in, axis=-1, keepdims=True)
                   + jnp.sum(qk, axis=-1, keepdims=True))
            hh = num / jnp.maximum(jnp.abs(den), jnp.exp(-m_t))
            hh = hh * lax.rsqrt(jnp.mean(hh * hh, axis=-1, keepdims=True) + EPS)
            hm_scr[rs, cs] = hh * mhg_ref[:, cs]

            wst = a_h - bcol + icol
            m_loc = jnp.max(wst, axis=0, keepdims=True)
            ke = kf_scr[rs, cs] * jnp.exp(wst - m_loc)
            c_loc = _dot(ke.T.astype(BF16), vh)
            n_loc = jnp.sum(ke, axis=0, keepdims=True)
            m_new = jnp.maximum(a_h + m_in, m_loc)
            s_prev = jnp.exp(a_h + m_in - m_new)
            s_loc = jnp.exp(m_loc - m_new)
            c_scr[hd] = s_prev * c_in + s_loc * c_loc
            n_scr[hd] = s_prev * n_in + s_loc * n_loc
            m_scr[hd] = jnp.broadcast_to(m_new, (1, LANES))

            w_sp = jnp.where(causal, ws_ref[hd], 0.0).astype(BF16)
            sv = _dot(w_sp, vn_scr[rs, cs]) + bs_ref[:, hd:hd + 1]
            mix_scr[rs, mw + hd * HEAD_DIM:mw + (hd + 1) * HEAD_DIM] = (
                ug_scr[rs, cs] * sv).astype(BF16)
        return carry

    lax.fori_loop(0, rows // CHUNK, chunk_body, 0)

    mix_scr[:, 0:mw] = (hm_scr[...] * _sigmoid(o_pre)).astype(BF16)
    mix = _dot(mix_scr[...], wout_ref[...])
    o_ref[...] = h_ref[...] + _rms(mix, g3_ref[...])


def _mixer(h, batch, seq, g2, g3, w_in, conv_w, conv_b, b_if, mh_norm_g,
           gmlp_ln_g, gmlp_ln_b, w_spatial, b_spatial, w_out):
    mw = MLSTM_WIDTH
    n_j = seq // MIX_ROWS
    gate0 = 4 * mw
    w_main = jnp.concatenate(
        [w_in[:, :gate0], w_in[:, gate0 + 2 * HEADS:]], axis=1).astype(BF16)
    w_if = jnp.zeros((D_MODEL, 2 * LANES), F32)
    w_if = w_if.at[:, 0:HEADS].set(w_in[:, gate0:gate0 + HEADS])
    w_if = w_if.at[:, LANES:LANES + HEADS].set(w_in[:, gate0 + HEADS:gate0 + 2 * HEADS])
    w_if = w_if.astype(BF16)
    b_i = jnp.zeros((1, LANES), F32).at[0, 0:HEADS].set(b_if[:HEADS])
    b_f = jnp.zeros((1, LANES), F32).at[0, 0:HEADS].set(b_if[HEADS:])
    row_spec = pl.BlockSpec((MIX_ROWS, D_MODEL), lambda b, j: (b * n_j + j, 0))
    args = (h, g2.reshape(1, D_MODEL), g3.reshape(1, D_MODEL), w_main, w_if, b_i, b_f,
            conv_w, conv_b.reshape(1, 2 * mw), mh_norm_g.reshape(1, mw),
            gmlp_ln_g.reshape(1, GMLP_WIDTH), gmlp_ln_b.reshape(1, GMLP_WIDTH),
            w_spatial, b_spatial.T, w_out.astype(BF16))
    in_specs = [row_spec] + [_const_spec(x.shape) for x in args[1:]]
    return pl.pallas_call(
        _mixer_kernel,
        grid=(batch, n_j),
        in_specs=in_specs,
        out_specs=row_spec,
        out_shape=jax.ShapeDtypeStruct(h.shape, F32),
        scratch_shapes=[
            pltpu.VMEM((SUBLANES, 2 * mw), F32),
            pltpu.VMEM((HEADS, HEAD_DIM, HEAD_DIM), F32),
            pltpu.VMEM((HEADS, 1, HEAD_DIM), F32),
            pltpu.VMEM((HEADS, 1, LANES), F32),
            pltpu.VMEM((MIX_ROWS, mw), BF16),
            pltpu.VMEM((MIX_ROWS, mw), BF16),
            pltpu.VMEM((MIX_ROWS, mw), F32),
            pltpu.VMEM((MIX_ROWS, mw), BF16),
            pltpu.VMEM((MIX_ROWS, LANES), F32),
            pltpu.VMEM((MIX_ROWS, LANES), F32),
            pltpu.VMEM((MIX_ROWS, mw), F32),
            pltpu.VMEM((MIX_ROWS, GMLP_WIDTH), BF16),
            pltpu.VMEM((MIX_ROWS, GMLP_WIDTH), F32),
            pltpu.VMEM((MIX_ROWS, mw + GMLP_WIDTH), BF16),
        ],
        compiler_params=pltpu.CompilerParams(
            dimension_semantics=("arbitrary", "arbitrary"),
            vmem_limit_bytes=VMEM_LIMIT),
        name="mixer",
    )(*args)


def _ple_kernel(h_ref, p_ref, g6_ref, g7_ref, wg_ref, wp_ref, o_ref):
    x = h_ref[...]
    gate = _sigmoid(_dot(_rms(x, g6_ref[...]).astype(BF16), wg_ref[...]))
    emb = _dot(p_ref[...].astype(BF16), wp_ref[...])
    o_ref[...] = x + _rms(gate * emb, g7_ref[...])


def _ple(h, p, g6, g7, w_gate, w_ple):
    m = h.shape[0]
    row_spec = pl.BlockSpec((PLE_ROWS, D_MODEL), lambda i: (i, 0))
    p_spec = pl.BlockSpec((PLE_ROWS, D_PLE), lambda i: (i, 0))
    return pl.pallas_call(
        _ple_kernel,
        grid=(m // PLE_ROWS,),
        in_specs=[row_spec, p_spec, _const_spec((1, D_MODEL)), _const_spec((1, D_MODEL)),
                  _const_spec(w_gate.shape), _const_spec(w_ple.shape)],
        out_specs=row_spec,
        out_shape=jax.ShapeDtypeStruct(h.shape, F32),
        compiler_params=pltpu.CompilerParams(
            dimension_semantics=("arbitrary",), vmem_limit_bytes=VMEM_LIMIT),
        name="ple",
    )(h, p, g6.reshape(1, D_MODEL), g7.reshape(1, D_MODEL),
      w_gate.astype(BF16), w_ple.astype(BF16))


def kernel(x, p, ffn1_gu, ffn1_down, ffn2_gu, ffn2_down, w_in, conv_w, conv_b, b_if,
           mh_norm_g, gmlp_ln_g, gmlp_ln_b, w_spatial, b_spatial, w_out, w_ple,
           w_ple_gate, norm_g):
    batch, seq, _ = x.shape
    depth = p.shape[0]
    h = x.reshape(batch * seq, D_MODEL)
    for i in range(depth):
        g = norm_g[i]
        h = _ffn(h, g[0], g[1], ffn1_gu[i], ffn1_down[i])
        h = _mixer(h, batch, seq, g[2], g[3], w_in[i], conv_w[i], conv_b[i], b_if[i],
                   mh_norm_g[i], gmlp_ln_g[i], gmlp_ln_b[i], w_spatial[i], b_spatial[i],
                   w_out[i])
        h = _ffn(h, g[4], g[5], ffn2_gu[i], ffn2_down[i])
        h = _ple(h, p[i].reshape(batch * seq, D_PLE), g[6], g[7], w_ple_gate[i], w_ple[i])
    return h.reshape(batch, seq, D_MODEL)
```

```python
import jax
import jax.numpy as jnp
import numpy as np
from jax import lax
from jax.experimental import pallas as pl
from jax.experimental.pallas import tpu as pltpu

D_MODEL = 1024
D_PLE = 256
D_FF = 2816
HEADS = 4
HEAD_DIM = 128
MLSTM_WIDTH = HEADS * HEAD_DIM
GMLP_WIDTH = HEADS * HEAD_DIM
CHUNK = 128
CONV_WIDTH = 4
EPS = 1e-6

LANES = 128
SUBLANES = 8

FFN_ROWS = 512
FFN_FC = 256
MIX_ROWS = 512
PLE_ROWS = 1024
VMEM_LIMIT = 56 * 1024 * 1024

F32 = jnp.float32
BF16 = jnp.bfloat16


def _rms(x, g):
    return x * lax.rsqrt(jnp.mean(x * x, axis=-1, keepdims=True) + EPS) * g


def _sigmoid(x):
    return 1.0 / (1.0 + jnp.exp(-x))


def _gelu_exact(x):
    return 0.5 * x * (1.0 + lax.erf(x * np.float32(np.sqrt(0.5))))


def _log_sigmoid(x):
    return -(jnp.maximum(-x, 0.0) + jnp.log1p(jnp.exp(-jnp.abs(x))))


def _dot(a, b):
    return jnp.dot(a, b, preferred_element_type=F32)


def _const_spec(shape):
    nd = len(shape)
    return pl.BlockSpec(shape, lambda *_: (0,) * nd, pipeline_mode=pl.Buffered(1))


def _ffn_kernel(x_ref, gin_ref, gout_ref, gu_ref, dn_ref, o_ref,
                a_scr, gua_scr, gub_scr, acc_scr):
    n_f = dn_ref.shape[0]
    fc = dn_ref.shape[1]
    a_scr[...] = _rms(x_ref[...], gin_ref[...]).astype(BF16)
    acc_scr[...] = jnp.zeros_like(acc_scr)

    def project(f, gu_scr):
        gu_scr[:, 0:fc] = _dot(a_scr[...], gu_ref[:, f * fc:(f + 1) * fc])
        gu_scr[:, fc:2 * fc] = _dot(a_scr[...], gu_ref[:, D_FF + f * fc:D_FF + (f + 1) * fc])

    def consume(f, gu_scr):
        g = gu_scr[:, 0:fc]
        u = gu_scr[:, fc:2 * fc]
        act = (g * _sigmoid(g)) * u
        acc_scr[...] += _dot(act.astype(BF16), dn_ref[f])

    bufs = (gua_scr, gub_scr)
    project(0, bufs[0])
    for f in range(n_f):
        if f + 1 < n_f:
            project(f + 1, bufs[(f + 1) % 2])
        consume(f, bufs[f % 2])
    o_ref[...] = x_ref[...] + 0.5 * _rms(acc_scr[...], gout_ref[...])


def _ffn(h, g_in, g_out, w_gu, w_down):
    m = h.shape[0]
    n_f = D_FF // FFN_FC
    gu = w_gu.astype(BF16)
    dn = w_down.reshape(n_f, FFN_FC, D_MODEL).astype(BF16)
    row_spec = pl.BlockSpec((FFN_ROWS, D_MODEL), lambda i: (i, 0))
    return pl.pallas_call(
        _ffn_kernel,
        grid=(m // FFN_ROWS,),
        in_specs=[row_spec, _const_spec((1, D_MODEL)), _const_spec((1, D_MODEL)),
                  _const_spec(gu.shape), _const_spec(dn.shape)],
        out_specs=row_spec,
        out_shape=jax.ShapeDtypeStruct(h.shape, F32),
        scratch_shapes=[pltpu.VMEM((FFN_ROWS, D_MODEL), BF16),
                        pltpu.VMEM((FFN_ROWS, 2 * FFN_FC), F32),
                        pltpu.VMEM((FFN_ROWS, 2 * FFN_FC), F32),
                        pltpu.VMEM((FFN_ROWS, D_MODEL), F32)],
        compiler_params=pltpu.CompilerParams(
            dimension_semantics=("arbitrary",), vmem_limit_bytes=VMEM_LIMIT),
        name="ffn",
    )(h, g_in.reshape(1, D_MODEL), g_out.reshape(1, D_MODEL), gu, dn)


_P_RT = (0, 1, 2)
_P_EINTER = (3, 4)
_P_ENM = (5, 6)
_P_EST = (7, 8)
_P_BLOCKS = 9


def _split2(x):
    hi = x.astype(BF16).astype(F32)
    lo = (x - hi).astype(BF16).astype(F32)
    return hi, lo


def _split3(x):
    hi = x.astype(BF16).astype(F32)
    r1 = x - hi
    mid = r1.astype(BF16).astype(F32)
    lo = (r1 - mid).astype(BF16).astype(F32)
    return hi, mid, lo


def _mixer_constants():
    s_in = np.arange(CHUNK)[:, None]
    s_out = np.arange(CHUNK)[None, :]
    ut = np.concatenate([(s_in <= s_out).astype(np.float32),
                         np.ones((CHUNK, CHUNK), np.float32)], axis=1)
    sel = np.zeros((HEADS, 2, LANES, 2 * LANES), np.float32)
    for hd in range(HEADS):
        for half, blocks in ((0, _P_RT), (1, _P_EINTER)):
            for blk in blocks:
                sel[hd, 0, blk * SUBLANES + hd, half * LANES:(half + 1) * LANES] = 1.0
        for half, blocks in ((0, _P_ENM), (1, _P_EST)):
            for blk in blocks:
                sel[hd, 1, blk * SUBLANES + hd, half * LANES:(half + 1) * LANES] = 1.0
    ones = np.ones((2 * LANES, LANES), np.float32)
    return (jnp.asarray(ut, BF16), jnp.asarray(sel, BF16), jnp.asarray(ones, BF16))


def _mixer_kernel(h_ref, g2_ref, g3_ref, wa_ref, wb_ref, wg_ref, bias_ref,
                  convw_ref, convb_ref, mhg_ref, lng_ref, lnb_ref, ws_ref, bs_ref,
                  wout_ref, ut_ref, sel_ref, ones_ref, o_ref,
                  tail_scr, caug_scr, m_scr,
                  q_scr, k_scr, kf_scr, v_scr, so_scr, vn_scr, ug_scr, mix_scr):
    rows = h_ref.shape[0]
    n_c = rows // CHUNK
    mw = MLSTM_WIDTH
    nr = SUBLANES * n_c

    @pl.when(pl.program_id(1) == 0)
    def _():
        tail_scr[...] = jnp.zeros_like(tail_scr)
        caug_scr[...] = jnp.zeros_like(caug_scr)
        m_scr[...] = jnp.zeros_like(m_scr)

    a = _rms(h_ref[...], g2_ref[...]).astype(BF16)

    gcol = _dot(a, wg_ref[...])
    qk_raw = _dot(a, wa_ref[:, 0:2 * mw])

    i_rows, f_rows = [], []
    for c in range(n_c):
        x = gcol[c * CHUNK:(c + 1) * CHUNK, :].T[0:SUBLANES, :] + bias_ref[...]
        i_rows.append(x)
        f_rows.append(pltpu.roll(_log_sigmoid(x), HEADS, axis=0))
    log_i = jnp.concatenate(i_rows, axis=0)
    log_f = jnp.concatenate(f_rows, axis=0)
    pieces = jnp.concatenate(_split3(log_f), axis=0).astype(BF16)
    r = _dot(pieces, ut_ref[...])
    r = r[0:nr] + r[nr:2 * nr] + r[2 * nr:3 * nr]
    b = r[:, 0:CHUNK]
    a_all = r[:, CHUNK:2 * CHUNK]

    v_scr[...] = _dot(a, wa_ref[:, 2 * mw:3 * mw]).astype(BF16)
    so_scr[...] = _sigmoid(_dot(a, wa_ref[:, 3 * mw:4 * mw]))

    cb = log_i - b
    lane = lax.broadcasted_iota(jnp.int32, (nr, CHUNK), 1)
    cmax = cb
    for sh in (1, 2, 4, 8, 16, 32, 64):
        cmax = jnp.where(lane >= sh, jnp.maximum(cmax, pltpu.roll(cmax, sh, axis=1)), cmax)
    cmax_last = jnp.broadcast_to(cmax[:, CHUNK - 1:CHUNK], (nr, CHUNK))
    m_loc = a_all + cmax_last
    e_state = jnp.exp(cb - cmax_last)

    m = m_scr[...]
    m_ins, s_prevs, s_locs = [], [], []
    for c in range(n_c):
        sl = slice(SUBLANES * c, SUBLANES * (c + 1))
        m_ins.append(m)
        m_new = jnp.maximum(a_all[sl] + m, m_loc[sl])
        s_prevs.append(jnp.exp(a_all[sl] + m - m_new))
        s_locs.append(jnp.exp(m_loc[sl] - m_new))
        m = m_new
    m_scr[...] = m
    inter = b + jnp.concatenate(m_ins, axis=0)
    m_t = jnp.maximum(inter, b + cmax)
    e_inter = jnp.exp(inter - m_t)
    e_negm = jnp.exp(-m_t)
    row_blocks = (*_split3(b - m_t), *_split2(e_inter), *_split2(e_negm), *_split2(e_state))
    pad = jnp.zeros((LANES - _P_BLOCKS * SUBLANES, CHUNK), F32)

    ug_scr[...] = _gelu_exact(_dot(a, wb_ref[:, 0:GMLP_WIDTH]))
    vg = _gelu_exact(_dot(a, wb_ref[:, GMLP_WIDTH:2 * GMLP_WIDTH]))
    mu = jnp.mean(vg, axis=-1, keepdims=True)
    var = jnp.mean(jnp.square(vg - mu), axis=-1, keepdims=True)
    vn = (vg - mu) * lax.rsqrt(var + EPS) * lng_ref[...] + lnb_ref[...]
    vn_scr[...] = vn.astype(BF16)

    tail = tail_scr[...]
    row8 = lax.broadcasted_iota(jnp.int32, (SUBLANES, 2 * mw), 0)
    conv = qk_raw * convw_ref[CONV_WIDTH - 1:CONV_WIDTH, :] + convb_ref[...]
    for k in range(1, CONV_WIDTH):
        xr = pltpu.roll(qk_raw, k, axis=0)
        tr = pltpu.roll(tail, k, axis=0)
        first = jnp.where(row8 < k, tr, xr[0:SUBLANES])
        xs = jnp.concatenate([first, xr[SUBLANES:]], axis=0)
        conv = conv + xs * convw_ref[CONV_WIDTH - 1 - k:CONV_WIDTH - k, :]
    tail_scr[...] = qk_raw[rows - SUBLANES:rows]
    qk_act = conv * _sigmoid(conv)
    q_scr[...] = qk_act[:, 0:mw].astype(BF16)
    k_scaled = qk_act[:, mw:2 * mw] * np.float32(HEAD_DIM ** -0.5)
    kf_scr[...] = k_scaled
    k_scr[...] = k_scaled.astype(BF16)

    r_i = lax.broadcasted_iota(jnp.int32, (CHUNK, CHUNK), 0)
    c_i = lax.broadcasted_iota(jnp.int32, (CHUNK, CHUNK), 1)
    causal = c_i <= r_i
    neg_inf = np.float32(-np.inf)
    ones_blk = jnp.ones((CHUNK, HEAD_DIM), BF16)
    w_sp = [jnp.where(causal, ws_ref[hd], 0.0).astype(BF16) for hd in range(HEADS)]
    b_sp = [jnp.broadcast_to(bs_ref[:, hd:hd + 1], (CHUNK, HEAD_DIM)) for hd in range(HEADS)]

    for c in range(n_c):
        sl = slice(SUBLANES * c, SUBLANES * (c + 1))
        rs = slice(c * CHUNK, (c + 1) * CHUNK)
        p_row = jnp.concatenate([blk[sl] for blk in row_blocks] + [pad], axis=0)
        p_col = p_row.T.astype(BF16)
        heads = range(HEADS)
        cols = [slice(hd * HEAD_DIM, (hd + 1) * HEAD_DIM) for hd in heads]
        bc_a = [_dot(p_col, sel_ref[hd, 0]) for hd in heads]
        bc_b = [_dot(p_col, sel_ref[hd, 1]) for hd in heads]
        qh = [q_scr[rs, cs] for cs in cols]
        v_aug = [jnp.concatenate([v_scr[rs, cs], ones_blk], axis=1) for cs in cols]
        s = [lax.dot_general(qh[hd], k_scr[rs, cols[hd]], (((1,), (1,)), ((), ())),
                             preferred_element_type=F32) for hd in heads]
        sv = [_dot(w_sp[hd], vn_scr[rs, cols[hd]]) for hd in heads]
        qk = []
        for hd in heads:
            cb_row = cb[SUBLANES * c + hd:SUBLANES * c + hd + 1, :]
            e_intra = jnp.exp(jnp.where(causal, bc_a[hd][:, 0:LANES] + cb_row, neg_inf))
            qk.append((s[hd] * e_intra).astype(BF16))
        pv = [_dot(qk[hd], v_aug[hd]) for hd in heads]
        c_aug = [caug_scr[hd] for hd in heads]
        qc = [_dot(qh[hd], c_aug[hd].astype(BF16)) for hd in heads]
        c_loc = []
        for hd in heads:
            ke = kf_scr[rs, cols[hd]] * bc_b[hd][:, LANES:2 * LANES]
            c_loc.append(_dot(ke.T.astype(BF16), v_aug[hd]))
        hh, ssum = [], []
        for hd in heads:
            e_inter_b = bc_a[hd][:, LANES:2 * LANES]
            num = e_inter_b * qc[hd][:, 0:LANES] + pv[hd][:, 0:LANES]
            den = e_inter_b * qc[hd][:, LANES:2 * LANES] + pv[hd][:, LANES:2 * LANES]
            hh.append(num / jnp.maximum(jnp.abs(den), bc_b[hd][:, 0:LANES]))
            sq = jnp.concatenate(_split2(hh[hd] * hh[hd]), axis=1).astype(BF16)
            ssum.append(_dot(sq, ones_ref[...]))
        for hd in heads:
            s_prev = s_prevs[c][hd:hd + 1, :]
            s_loc = s_locs[c][hd:hd + 1, :]
            caug_scr[hd] = (jnp.concatenate([s_prev, s_prev], axis=1) * c_aug[hd]
                            + jnp.concatenate([s_loc, s_loc], axis=1) * c_loc[hd])
            mix_scr[rs, mw + hd * HEAD_DIM:mw + (hd + 1) * HEAD_DIM] = (
                ug_scr[rs, cols[hd]] * (sv[hd] + b_sp[hd])).astype(BF16)
        for hd in heads:
            hn = (hh[hd] * lax.rsqrt(ssum[hd] * np.float32(1.0 / HEAD_DIM) + EPS)
                  * mhg_ref[:, cols[hd]])
            mix_scr[rs, cols[hd]] = (hn * so_scr[rs, cols[hd]]).astype(BF16)

    mix = _dot(mix_scr[...], wout_ref[...])
    o_ref[...] = h_ref[...] + _rms(mix, g3_ref[...])


def _mixer(h, batch, seq, g2, g3, w_in, conv_w, conv_b, b_if, mh_norm_g,
           gmlp_ln_g, gmlp_ln_b, w_spatial, b_spatial, w_out):
    mw = MLSTM_WIDTH
    n_j = seq // MIX_ROWS
    gate0 = 4 * mw
    w_a = w_in[:, :gate0].astype(BF16)
    w_b = w_in[:, gate0 + 2 * HEADS:].astype(BF16)
    w_g = jnp.zeros((D_MODEL, LANES), F32).at[:, 0:2 * HEADS].set(
        w_in[:, gate0:gate0 + 2 * HEADS]).astype(BF16)
    bias = jnp.broadcast_to(b_if.reshape(2 * HEADS, 1), (2 * HEADS, CHUNK))
    ut, sel, ones = _mixer_constants()
    row_spec = pl.BlockSpec((MIX_ROWS, D_MODEL), lambda b, j: (b * n_j + j, 0))
    args = (h, g2.reshape(1, D_MODEL), g3.reshape(1, D_MODEL), w_a, w_b, w_g, bias,
            conv_w, conv_b.reshape(1, 2 * mw), mh_norm_g.reshape(1, mw),
            gmlp_ln_g.reshape(1, GMLP_WIDTH), gmlp_ln_b.reshape(1, GMLP_WIDTH),
            w_spatial, b_spatial.T, w_out.astype(BF16), ut, sel, ones)
    in_specs = [row_spec] + [_const_spec(x.shape) for x in args[1:]]
    return pl.pallas_call(
        _mixer_kernel,
        grid=(batch, n_j),
        in_specs=in_specs,
        out_specs=row_spec,
        out_shape=jax.ShapeDtypeStruct(h.shape, F32),
        scratch_shapes=[
            pltpu.VMEM((SUBLANES, 2 * mw), F32),
            pltpu.VMEM((HEADS, HEAD_DIM, 2 * LANES), F32),
            pltpu.VMEM((SUBLANES, CHUNK), F32),
            pltpu.VMEM((MIX_ROWS, mw), BF16),
            pltpu.VMEM((MIX_ROWS, mw), BF16),
            pltpu.VMEM((MIX_ROWS, mw), F32),
            pltpu.VMEM((MIX_ROWS, mw), BF16),
            pltpu.VMEM((MIX_ROWS, mw), F32),
            pltpu.VMEM((MIX_ROWS, GMLP_WIDTH), BF16),
            pltpu.VMEM((MIX_ROWS, GMLP_WIDTH), F32),
            pltpu.VMEM((MIX_ROWS, mw + GMLP_WIDTH), BF16),
        ],
        compiler_params=pltpu.CompilerParams(
            dimension_semantics=("arbitrary", "arbitrary"),
            vmem_limit_bytes=VMEM_LIMIT),
        name="mixer",
    )(*args)


def _ple_kernel(h_ref, p_ref, g6_ref, g7_ref, wg_ref, wp_ref, o_ref):
    x = h_ref[...]
    gate = _sigmoid(_dot(_rms(x, g6_ref[...]).astype(BF16), wg_ref[...]))
    emb = _dot(p_ref[...].astype(BF16), wp_ref[...])
    o_ref[...] = x + _rms(gate * emb, g7_ref[...])


def _ple(h, p, g6, g7, w_gate, w_ple):
    m = h.shape[0]
    row_spec = pl.BlockSpec((PLE_ROWS, D_MODEL), lambda i: (i, 0))
    p_spec = pl.BlockSpec((PLE_ROWS, D_PLE), lambda i: (i, 0))
    return pl.pallas_call(
        _ple_kernel,
        grid=(m // PLE_ROWS,),
        in_specs=[row_spec, p_spec, _const_spec((1, D_MODEL)), _const_spec((1, D_MODEL)),
                  _const_spec(w_gate.shape), _const_spec(w_ple.shape)],
        out_specs=row_spec,
        out_shape=jax.ShapeDtypeStruct(h.shape, F32),
        compiler_params=pltpu.CompilerParams(
            dimension_semantics=("arbitrary",), vmem_limit_bytes=VMEM_LIMIT),
        name="ple",
    )(h, p, g6.reshape(1, D_MODEL), g7.reshape(1, D_MODEL),
      w_gate.astype(BF16), w_ple.astype(BF16))


def kernel(x, p, ffn1_gu, ffn1_down, ffn2_gu, ffn2_down, w_in, conv_w, conv_b, b_if,
           mh_norm_g, gmlp_ln_g, gmlp_ln_b, w_spatial, b_spatial, w_out, w_ple,
           w_ple_gate, norm_g):
    batch, seq, _ = x.shape
    depth = p.shape[0]
    h = x.reshape(batch * seq, D_MODEL)
    for i in range(depth):
        g = norm_g[i]
        h = _ffn(h, g[0], g[1], ffn1_gu[i], ffn1_down[i])
        h = _mixer(h, batch, seq, g[2], g[3], w_in[i], conv_w[i], conv_b[i], b_if[i],
                   mh_norm_g[i], gmlp_ln_g[i], gmlp_ln_b[i], w_spatial[i], b_spatial[i],
                   w_out[i])
        h = _ffn(h, g[4], g[5], ffn2_gu[i], ffn2_down[i])
        h = _ple(h, p[i].reshape(batch * seq, D_PLE), g[6], g[7], w_ple_gate[i], w_ple[i])
    return h.reshape(batch, seq, D_MODEL)
```

```python
import functools

import jax
import jax.numpy as jnp
import numpy as np
from jax import lax
from jax.experimental import pallas as pl
from jax.experimental.pallas import tpu as pltpu

D_MODEL = 1024
D_PLE = 256
D_FF = 2816
HEADS = 4
HEAD_DIM = 128
MLSTM_WIDTH = HEADS * HEAD_DIM
GMLP_WIDTH = HEADS * HEAD_DIM
CHUNK = 128
CONV_WIDTH = 4
EPS = 1e-6

LANES = 128
SUBLANES = 8

FFN_ROWS = 1024
FFN_SUB = 512
FFN_FC = 256
MIX_ROWS = 512
VMEM_LIMIT = 56 * 1024 * 1024

F32 = jnp.float32
BF16 = jnp.bfloat16


def _rms(x, g):
    return x * lax.rsqrt(jnp.mean(x * x, axis=-1, keepdims=True) + EPS) * g


def _sigmoid(x):
    return 1.0 / (1.0 + jnp.exp(-x))


def _gelu_exact(x):
    return 0.5 * x * (1.0 + lax.erf(x * np.float32(np.sqrt(0.5))))


def _log_sigmoid(x):
    return -(jnp.maximum(-x, 0.0) + jnp.log1p(jnp.exp(-jnp.abs(x))))


def _dot(a, b):
    return jnp.dot(a, b, preferred_element_type=F32)


def _const_spec(shape):
    nd = len(shape)
    return pl.BlockSpec(shape, lambda *_: (0,) * nd, pipeline_mode=pl.Buffered(1))


def _ffn_kernel(*refs, with_ple):
    if with_ple:
        (x_ref, gin_ref, gout_ref, gu_ref, dn_ref, p_ref, g6_ref, g7_ref, wg_ref, wp_ref,
         o_ref, a_scr, gua_scr, gub_scr, acc_scr) = refs
    else:
        x_ref, gin_ref, gout_ref, gu_ref, dn_ref, o_ref, a_scr, gua_scr, gub_scr, acc_scr = refs
    n_f = dn_ref.shape[0]
    fc = dn_ref.shape[1]
    bufs = (gua_scr, gub_scr)

    def ffn_rows(rs):
        a_scr[rs, :] = _rms(x_ref[rs, :], gin_ref[...]).astype(BF16)
        acc_scr[rs, :] = jnp.zeros((FFN_SUB, D_MODEL), F32)

        def project(f, gu_scr):
            gu_scr[rs, 0:fc] = _dot(a_scr[rs, :], gu_ref[:, f * fc:(f + 1) * fc])
            gu_scr[rs, fc:2 * fc] = _dot(a_scr[rs, :],
                                         gu_ref[:, D_FF + f * fc:D_FF + (f + 1) * fc])

        def consume(f, gu_scr):
            g = gu_scr[rs, 0:fc]
            u = gu_scr[rs, fc:2 * fc]
            act = (g * _sigmoid(g)) * u
            acc_scr[rs, :] += _dot(act.astype(BF16), dn_ref[f])

        project(0, bufs[0])
        for f in range(n_f):
            if f + 1 < n_f:
                project(f + 1, bufs[(f + 1) % 2])
            consume(f, bufs[f % 2])
        return x_ref[rs, :] + 0.5 * _rms(acc_scr[rs, :], gout_ref[...])

    def ple_rows(rs, h):
        emb = _dot(p_ref[rs, :].astype(BF16), wp_ref[...])
        gate = _sigmoid(_dot(_rms(h, g6_ref[...]).astype(BF16), wg_ref[...]))
        return h + _rms(gate * emb, g7_ref[...])

    subs = [slice(r * FFN_SUB, (r + 1) * FFN_SUB) for r in range(x_ref.shape[0] // FFN_SUB)]
    hs = [ffn_rows(rs) for rs in subs]
    if with_ple:
        hs = [ple_rows(rs, h) for rs, h in zip(subs, hs)]
    for rs, h in zip(subs, hs):
        o_ref[rs, :] = h


def _ffn(h, g_in, g_out, w_gu, w_down, ple=None):
    m = h.shape[0]
    n_f = D_FF // FFN_FC
    gu = w_gu.astype(BF16)
    dn = w_down.reshape(n_f, FFN_FC, D_MODEL).astype(BF16)
    row_spec = pl.BlockSpec((FFN_ROWS, D_MODEL), lambda i: (i, 0))
    vec_spec = _const_spec((1, D_MODEL))
    args = [h, g_in.reshape(1, D_MODEL), g_out.reshape(1, D_MODEL), gu, dn]
    in_specs = [row_spec, vec_spec, vec_spec, _const_spec(gu.shape), _const_spec(dn.shape)]
    if ple is not None:
        p, g6, g7, w_gate, w_ple = ple
        args += [p, g6.reshape(1, D_MODEL), g7.reshape(1, D_MODEL),
                 w_gate.astype(BF16), w_ple.astype(BF16)]
        in_specs += [pl.BlockSpec((FFN_ROWS, D_PLE), lambda i: (i, 0)), vec_spec, vec_spec,
                     _const_spec(w_gate.shape), _const_spec(w_ple.shape)]
    return pl.pallas_call(
        functools.partial(_ffn_kernel, with_ple=ple is not None),
        grid=(m // FFN_ROWS,),
        in_specs=in_specs,
        out_specs=row_spec,
        out_shape=jax.ShapeDtypeStruct(h.shape, F32),
        scratch_shapes=[pltpu.VMEM((FFN_ROWS, D_MODEL), BF16),
                        pltpu.VMEM((FFN_ROWS, 2 * FFN_FC), F32),
                        pltpu.VMEM((FFN_ROWS, 2 * FFN_FC), F32),
                        pltpu.VMEM((FFN_ROWS, D_MODEL), F32)],
        compiler_params=pltpu.CompilerParams(
            dimension_semantics=("arbitrary",), vmem_limit_bytes=VMEM_LIMIT),
        name="ffn_ple" if ple is not None else "ffn",
    )(*args)


_P_RT = (0, 1, 2)
_P_EINTER = (3, 4)
_P_ENM = (5, 6)
_P_EST = (7, 8)
_P_BLOCKS = 9


def _split2(x):
    hi = x.astype(BF16).astype(F32)
    lo = (x - hi).astype(BF16).astype(F32)
    return hi, lo


def _split3(x):
    hi = x.astype(BF16).astype(F32)
    r1 = x - hi
    mid = r1.astype(BF16).astype(F32)
    lo = (r1 - mid).astype(BF16).astype(F32)
    return hi, mid, lo


def _mixer_constants():
    s_in = np.arange(CHUNK)[:, None]
    s_out = np.arange(CHUNK)[None, :]
    ut = np.concatenate([(s_in <= s_out).astype(np.float32),
                         np.ones((CHUNK, CHUNK), np.float32)], axis=1)
    sel = np.zeros((HEADS, 2, LANES, 2 * LANES), np.float32)
    for hd in range(HEADS):
        for half, blocks in ((0, _P_RT), (1, _P_EINTER)):
            for blk in blocks:
                sel[hd, 0, blk * SUBLANES + hd, half * LANES:(half + 1) * LANES] = 1.0
        for half, blocks in ((0, _P_ENM), (1, _P_EST)):
            for blk in blocks:
                sel[hd, 1, blk * SUBLANES + hd, half * LANES:(half + 1) * LANES] = 1.0
    ones = np.ones((2 * LANES, LANES), np.float32)
    return (jnp.asarray(ut, BF16), jnp.asarray(sel, BF16), jnp.asarray(ones, BF16))


def _mixer_kernel(h_ref, g2_ref, g3_ref, wa_ref, wb_ref, wg_ref, bias_ref,
                  convw_ref, convb_ref, mhg_ref, lng_ref, lnb_ref, ws_ref, bs_ref,
                  wout_ref, ut_ref, sel_ref, ones_ref, o_ref,
                  tail_scr, caug_scr, m_scr,
                  q_scr, kf_scr, v_scr, so_scr, vn_scr, ug_scr, mix_scr):
    rows = h_ref.shape[0]
    n_c = rows // CHUNK
    mw = MLSTM_WIDTH
    nr = SUBLANES * n_c

    @pl.when(pl.program_id(1) == 0)
    def _():
        tail_scr[...] = jnp.zeros_like(tail_scr)
        caug_scr[...] = jnp.zeros_like(caug_scr)
        m_scr[...] = jnp.zeros_like(m_scr)

    a = _rms(h_ref[...], g2_ref[...]).astype(BF16)

    gcol = _dot(a, wg_ref[...])
    qk_raw = _dot(a, wa_ref[:, 0:2 * mw])

    i_rows, f_rows = [], []
    for c in range(n_c):
        x = gcol[c * CHUNK:(c + 1) * CHUNK, :].T[0:SUBLANES, :] + bias_ref[...]
        i_rows.append(x)
        f_rows.append(pltpu.roll(_log_sigmoid(x), HEADS, axis=0))
    log_i = jnp.concatenate(i_rows, axis=0)
    log_f = jnp.concatenate(f_rows, axis=0)
    pieces = jnp.concatenate(_split3(log_f), axis=0).astype(BF16)
    r = _dot(pieces, ut_ref[...])
    r = r[0:nr] + r[nr:2 * nr] + r[2 * nr:3 * nr]
    b = r[:, 0:CHUNK]
    a_all = r[:, CHUNK:2 * CHUNK]

    v_scr[...] = _dot(a, wa_ref[:, 2 * mw:3 * mw]).astype(BF16)
    so_scr[...] = _sigmoid(_dot(a, wa_ref[:, 3 * mw:4 * mw]))

    cb = log_i - b
    lane = lax.broadcasted_iota(jnp.int32, (nr, CHUNK), 1)
    cmax = cb
    for sh in (1, 2, 4, 8, 16, 32, 64):
        cmax = jnp.where(lane >= sh, jnp.maximum(cmax, pltpu.roll(cmax, sh, axis=1)), cmax)
    cmax_last = jnp.broadcast_to(cmax[:, CHUNK - 1:CHUNK], (nr, CHUNK))
    m_loc = a_all + cmax_last
    e_state = jnp.exp(cb - cmax_last)

    m = m_scr[...]
    m_ins, s_prevs, s_locs = [], [], []
    for c in range(n_c):
        sl = slice(SUBLANES * c, SUBLANES * (c + 1))
        m_ins.append(m)
        m_new = jnp.maximum(a_all[sl] + m, m_loc[sl])
        s_prevs.append(jnp.exp(a_all[sl] + m - m_new))
        s_locs.append(jnp.exp(m_loc[sl] - m_new))
        m = m_new
    m_scr[...] = m
    inter = b + jnp.concatenate(m_ins, axis=0)
    m_t = jnp.maximum(inter, b + cmax)
    e_inter = jnp.exp(inter - m_t)
    e_negm = jnp.exp(-m_t)
    row_blocks = (*_split3(b - m_t), *_split2(e_inter), *_split2(e_negm), *_split2(e_state))
    pad = jnp.zeros((LANES - _P_BLOCKS * SUBLANES, CHUNK), F32)

    ug_scr[...] = _gelu_exact(_dot(a, wb_ref[:, 0:GMLP_WIDTH]))
    vg = _gelu_exact(_dot(a, wb_ref[:, GMLP_WIDTH:2 * GMLP_WIDTH]))
    mu = jnp.mean(vg, axis=-1, keepdims=True)
    var = jnp.mean(jnp.square(vg - mu), axis=-1, keepdims=True)
    vn = (vg - mu) * lax.rsqrt(var + EPS) * lng_ref[...] + lnb_ref[...]
    vn_scr[...] = vn.astype(BF16)

    tail = tail_scr[...]
    row8 = lax.broadcasted_iota(jnp.int32, (SUBLANES, 2 * mw), 0)
    conv = qk_raw * convw_ref[CONV_WIDTH - 1:CONV_WIDTH, :] + convb_ref[...]
    for k in range(1, CONV_WIDTH):
        xr = pltpu.roll(qk_raw, k, axis=0)
        tr = pltpu.roll(tail, k, axis=0)
        first = jnp.where(row8 < k, tr, xr[0:SUBLANES])
        xs = jnp.concatenate([first, xr[SUBLANES:]], axis=0)
        conv = conv + xs * convw_ref[CONV_WIDTH - 1 - k:CONV_WIDTH - k, :]
    tail_scr[...] = qk_raw[rows - SUBLANES:rows]
    qk_act = conv * _sigmoid(conv)
    q_scr[...] = qk_act[:, 0:mw].astype(BF16)
    kf_scr[...] = qk_act[:, mw:2 * mw] * np.float32(HEAD_DIM ** -0.5)

    r_i = lax.broadcasted_iota(jnp.int32, (CHUNK, CHUNK), 0)
    c_i = lax.broadcasted_iota(jnp.int32, (CHUNK, CHUNK), 1)
    causal = c_i <= r_i
    neg_inf = np.float32(-np.inf)
    ones_blk = jnp.ones((CHUNK, HEAD_DIM), BF16)
    w_sp = [jnp.where(causal, ws_ref[hd], 0.0).astype(BF16) for hd in range(HEADS)]
    b_sp = [jnp.broadcast_to(bs_ref[:, hd:hd + 1], (CHUNK, HEAD_DIM)) for hd in range(HEADS)]

    chunks = range(n_c)
    heads = range(HEADS)
    pairs = [(c, hd) for c in chunks for hd in heads]
    rsl = [slice(c * CHUNK, (c + 1) * CHUNK) for c in chunks]
    cols = [slice(hd * HEAD_DIM, (hd + 1) * HEAD_DIM) for hd in heads]

    p_col = []
    for c in chunks:
        sl = slice(SUBLANES * c, SUBLANES * (c + 1))
        p_row = jnp.concatenate([blk[sl] for blk in row_blocks] + [pad], axis=0)
        p_col.append(p_row.T.astype(BF16))
    bc_a = {(c, hd): _dot(p_col[c], sel_ref[hd, 0]) for c, hd in pairs}
    bc_b = {(c, hd): _dot(p_col[c], sel_ref[hd, 1]) for c, hd in pairs}
    qh = {(c, hd): q_scr[rsl[c], cols[hd]] for c, hd in pairs}
    v_aug = {(c, hd): jnp.concatenate([v_scr[rsl[c], cols[hd]], ones_blk], axis=1)
             for c, hd in pairs}
    s = {(c, hd): _dot(qh[c, hd], kf_scr[rsl[c], cols[hd]].T.astype(BF16)) for c, hd in pairs}
    for c, hd in pairs:
        sv = _dot(w_sp[hd], vn_scr[rsl[c], cols[hd]]) + b_sp[hd]
        mix_scr[rsl[c], mw + hd * HEAD_DIM:mw + (hd + 1) * HEAD_DIM] = (
            ug_scr[rsl[c], cols[hd]] * sv).astype(BF16)
    pv = {}
    for c, hd in pairs:
        cb_row = cb[SUBLANES * c + hd:SUBLANES * c + hd + 1, :]
        e_intra = jnp.exp(jnp.where(causal, bc_a[c, hd][:, 0:LANES] + cb_row, neg_inf))
        qk = (s[c, hd] * e_intra).astype(BF16)
        pv[c, hd] = _dot(qk, v_aug[c, hd])
    c_loc = {}
    for c, hd in pairs:
        ke = kf_scr[rsl[c], cols[hd]] * bc_b[c, hd][:, LANES:2 * LANES]
        c_loc[c, hd] = _dot(ke.T.astype(BF16), v_aug[c, hd])
    qc = {}
    c_aug = [caug_scr[hd] for hd in heads]
    for c in chunks:
        for hd in heads:
            qc[c, hd] = _dot(qh[c, hd], c_aug[hd].astype(BF16))
            s_prev = s_prevs[c][hd:hd + 1, :]
            s_loc = s_locs[c][hd:hd + 1, :]
            c_aug[hd] = (jnp.concatenate([s_prev, s_prev], axis=1) * c_aug[hd]
                         + jnp.concatenate([s_loc, s_loc], axis=1) * c_loc[c, hd])
    for hd in heads:
        caug_scr[hd] = c_aug[hd]
    hh, ssum = {}, {}
    for c, hd in pairs:
        e_inter_b = bc_a[c, hd][:, LANES:2 * LANES]
        num = e_inter_b * qc[c, hd][:, 0:LANES] + pv[c, hd][:, 0:LANES]
        den = e_inter_b * qc[c, hd][:, LANES:2 * LANES] + pv[c, hd][:, LANES:2 * LANES]
        hh[c, hd] = num / jnp.maximum(jnp.abs(den), bc_b[c, hd][:, 0:LANES])
        sq = jnp.concatenate(_split2(hh[c, hd] * hh[c, hd]), axis=1).astype(BF16)
        ssum[c, hd] = _dot(sq, ones_ref[...])
    for c, hd in pairs:
        hn = (hh[c, hd] * lax.rsqrt(ssum[c, hd] * np.float32(1.0 / HEAD_DIM) + EPS)
              * mhg_ref[:, cols[hd]])
        mix_scr[rsl[c], cols[hd]] = (hn * so_scr[rsl[c], cols[hd]]).astype(BF16)

    mix = _dot(mix_scr[...], wout_ref[...])
    o_ref[...] = h_ref[...] + _rms(mix, g3_ref[...])


def _mixer(h, batch, seq, g2, g3, w_in, conv_w, conv_b, b_if, mh_norm_g,
           gmlp_ln_g, gmlp_ln_b, w_spatial, b_spatial, w_out):
    mw = MLSTM_WIDTH
    n_j = seq // MIX_ROWS
    gate0 = 4 * mw
    w_a = w_in[:, :gate0].astype(BF16)
    w_b = w_in[:, gate0 + 2 * HEADS:].astype(BF16)
    w_g = jnp.zeros((D_MODEL, LANES), F32).at[:, 0:2 * HEADS].set(
        w_in[:, gate0:gate0 + 2 * HEADS]).astype(BF16)
    bias = jnp.broadcast_to(b_if.reshape(2 * HEADS, 1), (2 * HEADS, CHUNK))
    ut, sel, ones = _mixer_constants()
    row_spec = pl.BlockSpec((MIX_ROWS, D_MODEL), lambda b, j: (b * n_j + j, 0))
    args = (h, g2.reshape(1, D_MODEL), g3.reshape(1, D_MODEL), w_a, w_b, w_g, bias,
            conv_w, conv_b.reshape(1, 2 * mw), mh_norm_g.reshape(1, mw),
            gmlp_ln_g.reshape(1, GMLP_WIDTH), gmlp_ln_b.reshape(1, GMLP_WIDTH),
            w_spatial, b_spatial.T, w_out.astype(BF16), ut, sel, ones)
    in_specs = [row_spec] + [_const_spec(x.shape) for x in args[1:]]
    return pl.pallas_call(
        _mixer_kernel,
        grid=(batch, n_j),
        in_specs=in_specs,
        out_specs=row_spec,
        out_shape=jax.ShapeDtypeStruct(h.shape, F32),
        scratch_shapes=[
            pltpu.VMEM((SUBLANES, 2 * mw), F32),
            pltpu.VMEM((HEADS, HEAD_DIM, 2 * LANES), F32),
            pltpu.VMEM((SUBLANES, CHUNK), F32),
            pltpu.VMEM((MIX_ROWS, mw), BF16),
            pltpu.VMEM((MIX_ROWS, mw), F32),
            pltpu.VMEM((MIX_ROWS, mw), BF16),
            pltpu.VMEM((MIX_ROWS, mw), F32),
            pltpu.VMEM((MIX_ROWS, GMLP_WIDTH), BF16),
            pltpu.VMEM((MIX_ROWS, GMLP_WIDTH), F32),
            pltpu.VMEM((MIX_ROWS, mw + GMLP_WIDTH), BF16),
        ],
        compiler_params=pltpu.CompilerParams(
            dimension_semantics=("arbitrary", "arbitrary"),
            vmem_limit_bytes=VMEM_LIMIT),
        name="mixer",
    )(*args)


def kernel(x, p, ffn1_gu, ffn1_down, ffn2_gu, ffn2_down, w_in, conv_w, conv_b, b_if,
           mh_norm_g, gmlp_ln_g, gmlp_ln_b, w_spatial, b_spatial, w_out, w_ple,
           w_ple_gate, norm_g):
    batch, seq, _ = x.shape
    depth = p.shape[0]
    h = x.reshape(batch * seq, D_MODEL)
    for i in range(depth):
        g = norm_g[i]
        h = _ffn(h, g[0], g[1], ffn1_gu[i], ffn1_down[i])
        h = _mixer(h, batch, seq, g[2], g[3], w_in[i], conv_w[i], conv_b[i], b_if[i],
                   mh_norm_g[i], gmlp_ln_g[i], gmlp_ln_b[i], w_spatial[i], b_spatial[i],
                   w_out[i])
        h = _ffn(h, g[4], g[5], ffn2_gu[i], ffn2_down[i],
                 ple=(p[i].reshape(batch * seq, D_PLE), g[6], g[7], w_ple_gate[i], w_ple[i]))
    return h.reshape(batch, seq, D_MODEL)
```

```python
import functools

import jax
import jax.numpy as jnp
import numpy as np
from jax import lax
from jax.experimental import pallas as pl
from jax.experimental.pallas import tpu as pltpu

D_MODEL = 1024
D_PLE = 256
D_FF = 2816
HEADS = 4
HEAD_DIM = 128
MLSTM_WIDTH = HEADS * HEAD_DIM
GMLP_WIDTH = HEADS * HEAD_DIM
CHUNK = 128
CONV_WIDTH = 4
EPS = 1e-6

LANES = 128
SUBLANES = 8

FFN_ROWS = 512
FFN_SUB = 512
FFN_FC = 256
MIX_ROWS = 512
VMEM_LIMIT = 56 * 1024 * 1024

F32 = jnp.float32
BF16 = jnp.bfloat16


def _rms(x, g):
    return x * lax.rsqrt(jnp.mean(x * x, axis=-1, keepdims=True) + EPS) * g


def _sigmoid(x):
    return 1.0 / (1.0 + jnp.exp(-x))


def _gelu_exact(x):
    return 0.5 * x * (1.0 + lax.erf(x * np.float32(np.sqrt(0.5))))


def _log_sigmoid(x):
    return -(jnp.maximum(-x, 0.0) + jnp.log1p(jnp.exp(-jnp.abs(x))))


def _dot(a, b):
    return jnp.dot(a, b, preferred_element_type=F32)


def _const_spec(shape):
    nd = len(shape)
    return pl.BlockSpec(shape, lambda *_: (0,) * nd, pipeline_mode=pl.Buffered(1))


def _ffn_kernel(*refs, with_ple):
    if with_ple:
        (x_ref, gin_ref, gout_ref, gu_ref, dn_ref, p_ref, g6_ref, g7_ref, wg_ref, wp_ref,
         o_ref, a_scr, gua_scr, gub_scr, acc_scr) = refs
    else:
        x_ref, gin_ref, gout_ref, gu_ref, dn_ref, o_ref, a_scr, gua_scr, gub_scr, acc_scr = refs
    n_f = dn_ref.shape[0]
    fc = dn_ref.shape[1]
    bufs = (gua_scr, gub_scr)

    def ffn_rows(rs):
        a_scr[rs, :] = _rms(x_ref[rs, :], gin_ref[...]).astype(BF16)
        acc_scr[rs, :] = jnp.zeros((FFN_SUB, D_MODEL), F32)

        def project(f, gu_scr):
            gate_w = gu_ref[:, f * fc:(f + 1) * fc].astype(BF16)
            up_w = gu_ref[:, D_FF + f * fc:D_FF + (f + 1) * fc].astype(BF16)
            gu_scr[rs, 0:fc] = _dot(a_scr[rs, :], gate_w)
            gu_scr[rs, fc:2 * fc] = _dot(a_scr[rs, :], up_w)

        def consume(f, gu_scr):
            g = gu_scr[rs, 0:fc]
            u = gu_scr[rs, fc:2 * fc]
            act = (g * _sigmoid(g)) * u
            acc_scr[rs, :] += _dot(act.astype(BF16), dn_ref[f].astype(BF16))

        project(0, bufs[0])
        for f in range(n_f):
            if f + 1 < n_f:
                project(f + 1, bufs[(f + 1) % 2])
            consume(f, bufs[f % 2])
        return x_ref[rs, :] + 0.5 * _rms(acc_scr[rs, :], gout_ref[...])

    def ple_rows(rs, h):
        emb = _dot(p_ref[rs, :].astype(BF16), wp_ref[...])
        gate = _sigmoid(_dot(_rms(h, g6_ref[...]).astype(BF16), wg_ref[...]))
        return h + _rms(gate * emb, g7_ref[...])

    subs = [slice(r * FFN_SUB, (r + 1) * FFN_SUB) for r in range(x_ref.shape[0] // FFN_SUB)]
    hs = [ffn_rows(rs) for rs in subs]
    if with_ple:
        hs = [ple_rows(rs, h) for rs, h in zip(subs, hs)]
    for rs, h in zip(subs, hs):
        o_ref[rs, :] = h


def _ffn(h, g_in, g_out, w_gu, w_down, ple=None):
    m = h.shape[0]
    n_f = D_FF // FFN_FC
    gu = w_gu
    dn = w_down.reshape(n_f, FFN_FC, D_MODEL)
    row_spec = pl.BlockSpec((FFN_ROWS, D_MODEL), lambda i: (i, 0))
    vec_spec = _const_spec((1, D_MODEL))
    args = [h, g_in.reshape(1, D_MODEL), g_out.reshape(1, D_MODEL), gu, dn]
    in_specs = [row_spec, vec_spec, vec_spec, _const_spec(gu.shape), _const_spec(dn.shape)]
    if ple is not None:
        p, g6, g7, w_gate, w_ple = ple
        args += [p, g6.reshape(1, D_MODEL), g7.reshape(1, D_MODEL),
                 w_gate.astype(BF16), w_ple.astype(BF16)]
        in_specs += [pl.BlockSpec((FFN_ROWS, D_PLE), lambda i: (i, 0)), vec_spec, vec_spec,
                     _const_spec(w_gate.shape), _const_spec(w_ple.shape)]
    return pl.pallas_call(
        functools.partial(_ffn_kernel, with_ple=ple is not None),
        grid=(m // FFN_ROWS,),
        in_specs=in_specs,
        out_specs=row_spec,
        out_shape=jax.ShapeDtypeStruct(h.shape, F32),
        scratch_shapes=[pltpu.VMEM((FFN_ROWS, D_MODEL), BF16),
                        pltpu.VMEM((FFN_ROWS, 2 * FFN_FC), F32),
                        pltpu.VMEM((FFN_ROWS, 2 * FFN_FC), F32),
                        pltpu.VMEM((FFN_ROWS, D_MODEL), F32)],
        compiler_params=pltpu.CompilerParams(
            dimension_semantics=("arbitrary",), vmem_limit_bytes=VMEM_LIMIT),
        name="ffn_ple" if ple is not None else "ffn",
    )(*args)


_P_RT = (0, 1, 2)
_P_EINTER = (3, 4)
_P_ENM = (5, 6)
_P_EST = (7, 8)
_P_BLOCKS = 9


def _split2(x):
    hi = x.astype(BF16).astype(F32)
    lo = (x - hi).astype(BF16).astype(F32)
    return hi, lo


def _split3(x):
    hi = x.astype(BF16).astype(F32)
    r1 = x - hi
    mid = r1.astype(BF16).astype(F32)
    lo = (r1 - mid).astype(BF16).astype(F32)
    return hi, mid, lo


def _mixer_constants():
    s_in = np.arange(CHUNK)[:, None]
    s_out = np.arange(CHUNK)[None, :]
    ut = np.concatenate([(s_in <= s_out).astype(np.float32),
                         np.ones((CHUNK, CHUNK), np.float32)], axis=1)
    sel = np.zeros((HEADS, 2, LANES, 2 * LANES), np.float32)
    for hd in range(HEADS):
        for half, blocks in ((0, _P_RT), (1, _P_EINTER)):
            for blk in blocks:
                sel[hd, 0, blk * SUBLANES + hd, half * LANES:(half + 1) * LANES] = 1.0
        for half, blocks in ((0, _P_ENM), (1, _P_EST)):
            for blk in blocks:
                sel[hd, 1, blk * SUBLANES + hd, half * LANES:(half + 1) * LANES] = 1.0
    ones = np.ones((2 * LANES, LANES), np.float32)
    return (jnp.asarray(ut, BF16), jnp.asarray(sel, BF16), jnp.asarray(ones, BF16))


def _mixer_kernel(h_ref, g2_ref, g3_ref, wa_ref, wb_ref, wg_ref, bias_ref,
                  convw_ref, convb_ref, mhg_ref, lng_ref, lnb_ref, ws_ref, bs_ref,
                  wout_ref, ut_ref, sel_ref, ones_ref, o_ref,
                  tail_scr, caug_scr, m_scr,
                  q_scr, kf_scr, v_scr, so_scr, vn_scr, ug_scr, mix_scr):
    rows = h_ref.shape[0]
    n_c = rows // CHUNK
    mw = MLSTM_WIDTH
    nr = SUBLANES * n_c

    @pl.when(pl.program_id(1) == 0)
    def _():
        tail_scr[...] = jnp.zeros_like(tail_scr)
        caug_scr[...] = jnp.zeros_like(caug_scr)
        m_scr[...] = jnp.zeros_like(m_scr)

    a = _rms(h_ref[...], g2_ref[...]).astype(BF16)

    gcol = _dot(a, wg_ref[...])
    qk_raw = _dot(a, wa_ref[:, 0:2 * mw].astype(BF16))

    i_rows, f_rows = [], []
    for c in range(n_c):
        x = gcol[c * CHUNK:(c + 1) * CHUNK, :].T[0:SUBLANES, :] + bias_ref[...]
        i_rows.append(x)
        f_rows.append(pltpu.roll(_log_sigmoid(x), HEADS, axis=0))
    log_i = jnp.concatenate(i_rows, axis=0)
    log_f = jnp.concatenate(f_rows, axis=0)
    pieces = jnp.concatenate(_split3(log_f), axis=0).astype(BF16)
    r = _dot(pieces, ut_ref[...])
    r = r[0:nr] + r[nr:2 * nr] + r[2 * nr:3 * nr]
    b = r[:, 0:CHUNK]
    a_all = r[:, CHUNK:2 * CHUNK]

    v_scr[...] = _dot(a, wa_ref[:, 2 * mw:3 * mw].astype(BF16)).astype(BF16)
    so_scr[...] = _sigmoid(_dot(a, wa_ref[:, 3 * mw:4 * mw].astype(BF16)))

    cb = log_i - b
    lane = lax.broadcasted_iota(jnp.int32, (nr, CHUNK), 1)
    cmax = cb
    for sh in (1, 2, 4, 8, 16, 32, 64):
        cmax = jnp.where(lane >= sh, jnp.maximum(cmax, pltpu.roll(cmax, sh, axis=1)), cmax)
    cmax_last = jnp.broadcast_to(cmax[:, CHUNK - 1:CHUNK], (nr, CHUNK))
    m_loc = a_all + cmax_last
    e_state = jnp.exp(cb - cmax_last)

    m = m_scr[...]
    m_ins, s_prevs, s_locs = [], [], []
    for c in range(n_c):
        sl = slice(SUBLANES * c, SUBLANES * (c + 1))
        m_ins.append(m)
        m_new = jnp.maximum(a_all[sl] + m, m_loc[sl])
        s_prevs.append(jnp.exp(a_all[sl] + m - m_new))
        s_locs.append(jnp.exp(m_loc[sl] - m_new))
        m = m_new
    m_scr[...] = m
    inter = b + jnp.concatenate(m_ins, axis=0)
    m_t = jnp.maximum(inter, b + cmax)
    e_inter = jnp.exp(inter - m_t)
    e_negm = jnp.exp(-m_t)
    row_blocks = (*_split3(b - m_t), *_split2(e_inter), *_split2(e_negm), *_split2(e_state))
    pad = jnp.zeros((LANES - _P_BLOCKS * SUBLANES, CHUNK), F32)

    ug_scr[...] = _gelu_exact(_dot(a, wb_ref[:, 0:GMLP_WIDTH]))
    vg = _gelu_exact(_dot(a, wb_ref[:, GMLP_WIDTH:2 * GMLP_WIDTH]))
    mu = jnp.mean(vg, axis=-1, keepdims=True)
    var = jnp.mean(jnp.square(vg - mu), axis=-1, keepdims=True)
    vn = (vg - mu) * lax.rsqrt(var + EPS) * lng_ref[...] + lnb_ref[...]
    vn_scr[...] = vn.astype(BF16)

    tail = tail_scr[...]
    row8 = lax.broadcasted_iota(jnp.int32, (SUBLANES, 2 * mw), 0)
    conv = qk_raw * convw_ref[CONV_WIDTH - 1:CONV_WIDTH, :] + convb_ref[...]
    for k in range(1, CONV_WIDTH):
        xr = pltpu.roll(qk_raw, k, axis=0)
        tr = pltpu.roll(tail, k, axis=0)
        first = jnp.where(row8 < k, tr, xr[0:SUBLANES])
        xs = jnp.concatenate([first, xr[SUBLANES:]], axis=0)
        conv = conv + xs * convw_ref[CONV_WIDTH - 1 - k:CONV_WIDTH - k, :]
    tail_scr[...] = qk_raw[rows - SUBLANES:rows]
    qk_act = conv * _sigmoid(conv)
    q_scr[...] = qk_act[:, 0:mw].astype(BF16)
    kf_scr[...] = qk_act[:, mw:2 * mw] * np.float32(HEAD_DIM ** -0.5)

    r_i = lax.broadcasted_iota(jnp.int32, (CHUNK, CHUNK), 0)
    c_i = lax.broadcasted_iota(jnp.int32, (CHUNK, CHUNK), 1)
    causal = c_i <= r_i
    neg_inf = np.float32(-np.inf)
    ones_blk = jnp.ones((CHUNK, HEAD_DIM), BF16)
    w_sp = [jnp.where(causal, ws_ref[hd], 0.0).astype(BF16) for hd in range(HEADS)]
    b_sp = [jnp.broadcast_to(bs_ref[:, hd:hd + 1], (CHUNK, HEAD_DIM)) for hd in range(HEADS)]

    chunks = range(n_c)
    heads = range(HEADS)
    pairs = [(c, hd) for c in chunks for hd in heads]
    rsl = [slice(c * CHUNK, (c + 1) * CHUNK) for c in chunks]
    cols = [slice(hd * HEAD_DIM, (hd + 1) * HEAD_DIM) for hd in heads]

    p_col = []
    for c in chunks:
        sl = slice(SUBLANES * c, SUBLANES * (c + 1))
        p_row = jnp.concatenate([blk[sl] for blk in row_blocks] + [pad], axis=0)
        p_col.append(p_row.T.astype(BF16))
    bc_a = {(c, hd): _dot(p_col[c], sel_ref[hd, 0]) for c, hd in pairs}
    bc_b = {(c, hd): _dot(p_col[c], sel_ref[hd, 1]) for c, hd in pairs}
    qh = {(c, hd): q_scr[rsl[c], cols[hd]] for c, hd in pairs}
    v_aug = {(c, hd): jnp.concatenate([v_scr[rsl[c], cols[hd]], ones_blk], axis=1)
             for c, hd in pairs}
    s = {(c, hd): _dot(qh[c, hd], kf_scr[rsl[c], cols[hd]].T.astype(BF16)) for c, hd in pairs}
    for c, hd in pairs:
        sv = _dot(w_sp[hd], vn_scr[rsl[c], cols[hd]]) + b_sp[hd]
        mix_scr[rsl[c], mw + hd * HEAD_DIM:mw + (hd + 1) * HEAD_DIM] = (
            ug_scr[rsl[c], cols[hd]] * sv).astype(BF16)
    pv = {}
    for c, hd in pairs:
        cb_row = cb[SUBLANES * c + hd:SUBLANES * c + hd + 1, :]
        e_intra = jnp.exp(jnp.where(causal, bc_a[c, hd][:, 0:LANES] + cb_row, neg_inf))
        qk = (s[c, hd] * e_intra).astype(BF16)
        pv[c, hd] = _dot(qk, v_aug[c, hd])
    c_loc = {}
    for c, hd in pairs:
        ke = kf_scr[rsl[c], cols[hd]] * bc_b[c, hd][:, LANES:2 * LANES]
        c_loc[c, hd] = _dot(ke.T.astype(BF16), v_aug[c, hd])
    qc = {}
    c_aug = [caug_scr[hd] for hd in heads]
    for c in chunks:
        for hd in heads:
            qc[c, hd] = _dot(qh[c, hd], c_aug[hd].astype(BF16))
            s_prev = s_prevs[c][hd:hd + 1, :]
            s_loc = s_locs[c][hd:hd + 1, :]
            c_aug[hd] = (jnp.concatenate([s_prev, s_prev], axis=1) * c_aug[hd]
                         + jnp.concatenate([s_loc, s_loc], axis=1) * c_loc[c, hd])
    for hd in heads:
        caug_scr[hd] = c_aug[hd]
    hh, ssum = {}, {}
    for c, hd in pairs:
        e_inter_b = bc_a[c, hd][:, LANES:2 * LANES]
        num = e_inter_b * qc[c, hd][:, 0:LANES] + pv[c, hd][:, 0:LANES]
        den = e_inter_b * qc[c, hd][:, LANES:2 * LANES] + pv[c, hd][:, LANES:2 * LANES]
        hh[c, hd] = num / jnp.maximum(jnp.abs(den), bc_b[c, hd][:, 0:LANES])
        sq = jnp.concatenate(_split2(hh[c, hd] * hh[c, hd]), axis=1).astype(BF16)
        ssum[c, hd] = _dot(sq, ones_ref[...])
    for c, hd in pairs:
        hn = (hh[c, hd] * lax.rsqrt(ssum[c, hd] * np.float32(1.0 / HEAD_DIM) + EPS)
              * mhg_ref[:, cols[hd]])
        mix_scr[rsl[c], cols[hd]] = (hn * so_scr[rsl[c], cols[hd]]).astype(BF16)

    mix = _dot(mix_scr[...], wout_ref[...].astype(BF16))
    o_ref[...] = h_ref[...] + _rms(mix, g3_ref[...])


def _mixer(h, batch, seq, g2, g3, w_in, conv_w, conv_b, b_if, mh_norm_g,
           gmlp_ln_g, gmlp_ln_b, w_spatial, b_spatial, w_out):
    mw = MLSTM_WIDTH
    n_j = seq // MIX_ROWS
    gate0 = 4 * mw
    w_b = w_in[:, gate0 + 2 * HEADS:].astype(BF16)
    w_g = jnp.zeros((D_MODEL, LANES), F32).at[:, 0:2 * HEADS].set(
        w_in[:, gate0:gate0 + 2 * HEADS]).astype(BF16)
    bias = jnp.broadcast_to(b_if.reshape(2 * HEADS, 1), (2 * HEADS, CHUNK))
    ut, sel, ones = _mixer_constants()
    row_spec = pl.BlockSpec((MIX_ROWS, D_MODEL), lambda b, j: (b * n_j + j, 0))
    args = (h, g2.reshape(1, D_MODEL), g3.reshape(1, D_MODEL), w_in, w_b, w_g, bias,
            conv_w, conv_b.reshape(1, 2 * mw), mh_norm_g.reshape(1, mw),
            gmlp_ln_g.reshape(1, GMLP_WIDTH), gmlp_ln_b.reshape(1, GMLP_WIDTH),
            w_spatial, b_spatial.T, w_out, ut, sel, ones)
    in_specs = [row_spec] + [_const_spec(x.shape) for x in args[1:]]
    in_specs[3] = _const_spec((D_MODEL, gate0))
    return pl.pallas_call(
        _mixer_kernel,
        grid=(batch, n_j),
        in_specs=in_specs,
        out_specs=row_spec,
        out_shape=jax.ShapeDtypeStruct(h.shape, F32),
        scratch_shapes=[
            pltpu.VMEM((SUBLANES, 2 * mw), F32),
            pltpu.VMEM((HEADS, HEAD_DIM, 2 * LANES), F32),
            pltpu.VMEM((SUBLANES, CHUNK), F32),
            pltpu.VMEM((MIX_ROWS, mw), BF16),
            pltpu.VMEM((MIX_ROWS, mw), F32),
            pltpu.VMEM((MIX_ROWS, mw), BF16),
            pltpu.VMEM((MIX_ROWS, mw), F32),
            pltpu.VMEM((MIX_ROWS, GMLP_WIDTH), BF16),
            pltpu.VMEM((MIX_ROWS, GMLP_WIDTH), F32),
            pltpu.VMEM((MIX_ROWS, mw + GMLP_WIDTH), BF16),
        ],
        compiler_params=pltpu.CompilerParams(
            dimension_semantics=("arbitrary", "arbitrary"),
            vmem_limit_bytes=VMEM_LIMIT),
        name="mixer",
    )(*args)


def kernel(x, p, ffn1_gu, ffn1_down, ffn2_gu, ffn2_down, w_in, conv_w, conv_b, b_if,
           mh_norm_g, gmlp_ln_g, gmlp_ln_b, w_spatial, b_spatial, w_out, w_ple,
           w_ple_gate, norm_g):
    batch, seq, _ = x.shape
    depth = p.shape[0]
    h = x.reshape(batch * seq, D_MODEL)
    for i in range(depth):
        g = norm_g[i]
        h = _ffn(h, g[0], g[1], ffn1_gu[i], ffn1_down[i])
        h = _mixer(h, batch, seq, g[2], g[3], w_in[i], conv_w[i], conv_b[i], b_if[i],
                   mh_norm_g[i], gmlp_ln_g[i], gmlp_ln_b[i], w_spatial[i], b_spatial[i],
                   w_out[i])
        h = _ffn(h, g[4], g[5], ffn2_gu[i], ffn2_down[i],
                 ple=(p[i].reshape(batch * seq, D_PLE), g[6], g[7], w_ple_gate[i], w_ple[i]))
    return h.reshape(batch, seq, D_MODEL)
```

```python
import functools

import jax
import jax.numpy as jnp
import numpy as np
from jax import lax
from jax.experimental import pallas as pl
from jax.experimental.pallas import tpu as pltpu

D_MODEL = 1024
D_PLE = 256
D_FF = 2816
HEADS = 4
HEAD_DIM = 128
MLSTM_WIDTH = HEADS * HEAD_DIM
GMLP_WIDTH = HEADS * HEAD_DIM
CHUNK = 128
CONV_WIDTH = 4
EPS = 1e-6

LANES = 128
SUBLANES = 8

FFN_ROWS = 512
SLAB = 128
FFN_FC = 256
MIX_ROWS = 512
VMEM_LIMIT = 56 * 1024 * 1024

F32 = jnp.float32
BF16 = jnp.bfloat16


def _rms(x, g):
    return x * lax.rsqrt(jnp.mean(x * x, axis=-1, keepdims=True) + EPS) * g


def _sigmoid(x):
    return 1.0 / (1.0 + jnp.exp(-x))


def _gelu_exact(x):
    return 0.5 * x * (1.0 + lax.erf(x * np.float32(np.sqrt(0.5))))


def _log_sigmoid(x):
    return -(jnp.maximum(-x, 0.0) + jnp.log1p(jnp.exp(-jnp.abs(x))))


def _dot(a, b):
    return jnp.dot(a, b, preferred_element_type=F32)


def _const_spec(shape):
    nd = len(shape)
    return pl.BlockSpec(shape, lambda *_: (0,) * nd, pipeline_mode=pl.Buffered(1))


def _ffn_kernel(*refs, with_ple):
    if with_ple:
        (x_ref, gin_ref, gout_ref, gu_ref, dn_ref, p_ref, g6_ref, g7_ref, wg_ref, wp_ref,
         o_ref, a_scr, gua_scr, gub_scr, acc_scr) = refs
    else:
        x_ref, gin_ref, gout_ref, gu_ref, dn_ref, o_ref, a_scr, gua_scr, gub_scr, acc_scr = refs
    n_f = dn_ref.shape[0]
    fc = dn_ref.shape[1]
    rows = x_ref.shape[0]
    bufs = (gua_scr, gub_scr)
    whole = [slice(0, rows)]
    slabs = [slice(r, r + SLAB) for r in range(0, rows, SLAB)]

    def project(f, gu_scr, row_sets):
        gate_w = gu_ref[:, f * fc:(f + 1) * fc].astype(BF16)
        up_w = gu_ref[:, D_FF + f * fc:D_FF + (f + 1) * fc].astype(BF16)
        for rs in row_sets:
            gu_scr[rs, 0:fc] = _dot(a_scr[rs, :], gate_w)
            gu_scr[rs, fc:2 * fc] = _dot(a_scr[rs, :], up_w)

    def down(f, gu_scr, rs):
        g = gu_scr[rs, 0:fc]
        u = gu_scr[rs, fc:2 * fc]
        act = (g * _sigmoid(g)) * u
        return _dot(act.astype(BF16), dn_ref[f].astype(BF16))

    a_scr[...] = _rms(x_ref[...], gin_ref[...]).astype(BF16)
    project(0, bufs[0], slabs)
    for f in range(n_f - 1):
        project(f + 1, bufs[(f + 1) % 2], whole)
        if f == 0:
            acc_scr[...] = down(f, bufs[f % 2], whole[0])
        else:
            acc_scr[...] += down(f, bufs[f % 2], whole[0])
    hs = []
    for rs in slabs:
        acc = acc_scr[rs, :] + down(n_f - 1, bufs[(n_f - 1) % 2], rs)
        hs.append(x_ref[rs, :] + 0.5 * _rms(acc, gout_ref[...]))
    if with_ple:
        embs = [_dot(p_ref[rs, :].astype(BF16), wp_ref[...]) for rs in slabs]
        gates = [_sigmoid(_dot(_rms(h, g6_ref[...]).astype(BF16), wg_ref[...])) for h in hs]
        hs = [h + _rms(gate * emb, g7_ref[...]) for h, gate, emb in zip(hs, gates, embs)]
    for rs, h in zip(slabs, hs):
        o_ref[rs, :] = h


def _ffn(h, g_in, g_out, w_gu, w_down, ple=None):
    m = h.shape[0]
    n_f = D_FF // FFN_FC
    gu = w_gu
    dn = w_down.reshape(n_f, FFN_FC, D_MODEL)
    row_spec = pl.BlockSpec((FFN_ROWS, D_MODEL), lambda i: (i, 0))
    vec_spec = _const_spec((1, D_MODEL))
    args = [h, g_in.reshape(1, D_MODEL), g_out.reshape(1, D_MODEL), gu, dn]
    in_specs = [row_spec, vec_spec, vec_spec, _const_spec(gu.shape), _const_spec(dn.shape)]
    if ple is not None:
        p, g6, g7, w_gate, w_ple = ple
        args += [p, g6.reshape(1, D_MODEL), g7.reshape(1, D_MODEL),
                 w_gate.astype(BF16), w_ple.astype(BF16)]
        in_specs += [pl.BlockSpec((FFN_ROWS, D_PLE), lambda i: (i, 0)), vec_spec, vec_spec,
                     _const_spec(w_gate.shape), _const_spec(w_ple.shape)]
    return pl.pallas_call(
        functools.partial(_ffn_kernel, with_ple=ple is not None),
        grid=(m // FFN_ROWS,),
        in_specs=in_specs,
        out_specs=row_spec,
        out_shape=jax.ShapeDtypeStruct(h.shape, F32),
        scratch_shapes=[pltpu.VMEM((FFN_ROWS, D_MODEL), BF16),
                        pltpu.VMEM((FFN_ROWS, 2 * FFN_FC), F32),
                        pltpu.VMEM((FFN_ROWS, 2 * FFN_FC), F32),
                        pltpu.VMEM((FFN_ROWS, D_MODEL), F32)],
        compiler_params=pltpu.CompilerParams(
            dimension_semantics=("arbitrary",), vmem_limit_bytes=VMEM_LIMIT),
        name="ffn_ple" if ple is not None else "ffn",
    )(*args)


_P_RT = 0
_P_EINTER = 1
_P_ENM = 2
_P_BLOCKS = 3


def _split2(x):
    hi = x.astype(BF16).astype(F32)
    lo = (x - hi).astype(BF16).astype(F32)
    return hi, lo


def _split3(x):
    hi = x.astype(BF16).astype(F32)
    r1 = x - hi
    mid = r1.astype(BF16).astype(F32)
    lo = (r1 - mid).astype(BF16).astype(F32)
    return hi, mid, lo


def _mixer_constants():
    s_in = np.arange(CHUNK)[:, None]
    s_out = np.arange(CHUNK)[None, :]
    ut = np.concatenate([(s_in <= s_out).astype(np.float32),
                         np.ones((CHUNK, CHUNK), np.float32)], axis=1)
    ones = np.ones((2 * LANES, LANES), np.float32)
    return jnp.asarray(ut, BF16), jnp.asarray(ones, BF16)


def _mixer_kernel(h_ref, g2_ref, g3_ref, wa_ref, wb_ref, wg_ref, bias_ref,
                  convw_ref, convb_ref, mhg_ref, lng_ref, lnb_ref, ws_ref, bs_ref,
                  wout_ref, ut_ref, ones_ref, o_ref,
                  tail_scr, caug_scr, m_scr,
                  q_scr, kf_scr, v_scr, so_scr, vn_scr, ug_scr, mix_scr):
    rows = h_ref.shape[0]
    n_c = rows // CHUNK
    mw = MLSTM_WIDTH
    nr = SUBLANES * n_c

    @pl.when(pl.program_id(1) == 0)
    def _():
        tail_scr[...] = jnp.zeros_like(tail_scr)
        caug_scr[...] = jnp.zeros_like(caug_scr)
        m_scr[...] = jnp.zeros_like(m_scr)

    a = _rms(h_ref[...], g2_ref[...]).astype(BF16)

    gcol = _dot(a, wg_ref[...])
    qk_raw = _dot(a, wa_ref[:, 0:2 * mw].astype(BF16))

    i_rows, f_rows = [], []
    for c in range(n_c):
        x = gcol[c * CHUNK:(c + 1) * CHUNK, :].T[0:SUBLANES, :] + bias_ref[...]
        i_rows.append(x)
        f_rows.append(pltpu.roll(_log_sigmoid(x), HEADS, axis=0))
    log_i = jnp.concatenate(i_rows, axis=0)
    log_f = jnp.concatenate(f_rows, axis=0)
    pieces = jnp.concatenate(_split3(log_f), axis=0).astype(BF16)
    r = _dot(pieces, ut_ref[...])
    r = r[0:nr] + r[nr:2 * nr] + r[2 * nr:3 * nr]
    b = r[:, 0:CHUNK]
    a_all = r[:, CHUNK:2 * CHUNK]

    v_scr[...] = _dot(a, wa_ref[:, 2 * mw:3 * mw].astype(BF16)).astype(BF16)
    so_scr[...] = _sigmoid(_dot(a, wa_ref[:, 3 * mw:4 * mw].astype(BF16)))

    cb = log_i - b
    lane = lax.broadcasted_iota(jnp.int32, (nr, CHUNK), 1)
    cmax = cb
    for sh in (1, 2, 4, 8, 16, 32, 64):
        cmax = jnp.where(lane >= sh, jnp.maximum(cmax, pltpu.roll(cmax, sh, axis=1)), cmax)
    cmax_last = jnp.broadcast_to(cmax[:, CHUNK - 1:CHUNK], (nr, CHUNK))
    m_loc = a_all + cmax_last
    e_state = jnp.exp(cb - cmax_last)

    m = m_scr[...]
    m_ins, s_prevs, s_locs = [], [], []
    for c in range(n_c):
        sl = slice(SUBLANES * c, SUBLANES * (c + 1))
        m_ins.append(m)
        m_new = jnp.maximum(a_all[sl] + m, m_loc[sl])
        s_prevs.append(jnp.exp(a_all[sl] + m - m_new))
        s_locs.append(jnp.exp(m_loc[sl] - m_new))
        m = m_new
    m_scr[...] = m
    inter = b + jnp.concatenate(m_ins, axis=0)
    m_t = jnp.maximum(inter, b + cmax)
    e_inter = jnp.exp(inter - m_t)
    e_negm = jnp.exp(-m_t)
    row_blocks = (b - m_t, e_inter, e_negm)
    pad = jnp.zeros((LANES - _P_BLOCKS * SUBLANES, CHUNK), F32)

    ug_scr[...] = _gelu_exact(_dot(a, wb_ref[:, 0:GMLP_WIDTH]))
    vg = _gelu_exact(_dot(a, wb_ref[:, GMLP_WIDTH:2 * GMLP_WIDTH]))
    mu = jnp.mean(vg, axis=-1, keepdims=True)
    var = jnp.mean(jnp.square(vg - mu), axis=-1, keepdims=True)
    vn = (vg - mu) * lax.rsqrt(var + EPS) * lng_ref[...] + lnb_ref[...]
    vn_scr[...] = vn.astype(BF16)

    tail = tail_scr[...]
    row8 = lax.broadcasted_iota(jnp.int32, (SUBLANES, 2 * mw), 0)
    conv = qk_raw * convw_ref[CONV_WIDTH - 1:CONV_WIDTH, :] + convb_ref[...]
    for k in range(1, CONV_WIDTH):
        xr = pltpu.roll(qk_raw, k, axis=0)
        tr = pltpu.roll(tail, k, axis=0)
        first = jnp.where(row8 < k, tr, xr[0:SUBLANES])
        xs = jnp.concatenate([first, xr[SUBLANES:]], axis=0)
        conv = conv + xs * convw_ref[CONV_WIDTH - 1 - k:CONV_WIDTH - k, :]
    tail_scr[...] = qk_raw[rows - SUBLANES:rows]
    qk_act = conv * _sigmoid(conv)
    q_scr[...] = qk_act[:, 0:mw].astype(BF16)
    kf_scr[...] = qk_act[:, mw:2 * mw] * np.float32(HEAD_DIM ** -0.5)

    r_i = lax.broadcasted_iota(jnp.int32, (CHUNK, CHUNK), 0)
    c_i = lax.broadcasted_iota(jnp.int32, (CHUNK, CHUNK), 1)
    causal = c_i <= r_i
    neg_inf = np.float32(-np.inf)
    ones_blk = jnp.ones((CHUNK, HEAD_DIM), BF16)
    w_sp = [jnp.where(causal, ws_ref[hd], 0.0).astype(BF16) for hd in range(HEADS)]
    b_sp = [jnp.broadcast_to(bs_ref[:, hd:hd + 1], (CHUNK, HEAD_DIM)) for hd in range(HEADS)]

    chunks = range(n_c)
    heads = range(HEADS)
    pairs = [(c, hd) for c in chunks for hd in heads]
    rsl = [slice(c * CHUNK, (c + 1) * CHUNK) for c in chunks]
    cols = [slice(hd * HEAD_DIM, (hd + 1) * HEAD_DIM) for hd in heads]

    p_col = []
    for c in chunks:
        sl = slice(SUBLANES * c, SUBLANES * (c + 1))
        p_row = jnp.concatenate([blk[sl] for blk in row_blocks] + [pad], axis=0)
        p_col.append(p_row.T)

    def lane_bcast(c, blk, hd):
        j = blk * SUBLANES + hd
        return jnp.broadcast_to(p_col[c][:, j:j + 1], (CHUNK, LANES))

    qh = {(c, hd): q_scr[rsl[c], cols[hd]] for c, hd in pairs}
    v_aug = {(c, hd): jnp.concatenate([v_scr[rsl[c], cols[hd]], ones_blk], axis=1)
             for c, hd in pairs}
    k_t = {(c, hd): kf_scr[rsl[c], cols[hd]].T for c, hd in pairs}
    s = {(c, hd): _dot(qh[c, hd], k_t[c, hd].astype(BF16)) for c, hd in pairs}
    for c, hd in pairs:
        sv = _dot(w_sp[hd], vn_scr[rsl[c], cols[hd]]) + b_sp[hd]
        mix_scr[rsl[c], mw + hd * HEAD_DIM:mw + (hd + 1) * HEAD_DIM] = (
            ug_scr[rsl[c], cols[hd]] * sv).astype(BF16)
    pv = {}
    for c, hd in pairs:
        cb_row = cb[SUBLANES * c + hd:SUBLANES * c + hd + 1, :]
        e_intra = jnp.exp(jnp.where(causal, lane_bcast(c, _P_RT, hd) + cb_row, neg_inf))
        qk = (s[c, hd] * e_intra).astype(BF16)
        pv[c, hd] = _dot(qk, v_aug[c, hd])
    c_loc = {}
    for c, hd in pairs:
        es_row = e_state[SUBLANES * c + hd:SUBLANES * c + hd + 1, :]
        c_loc[c, hd] = _dot((k_t[c, hd] * es_row).astype(BF16), v_aug[c, hd])
    qc = {}
    c_aug = [caug_scr[hd] for hd in heads]
    for c in chunks:
        for hd in heads:
            qc[c, hd] = _dot(qh[c, hd], c_aug[hd].astype(BF16))
            s_prev = s_prevs[c][hd:hd + 1, :]
            s_loc = s_locs[c][hd:hd + 1, :]
            c_aug[hd] = (jnp.concatenate([s_prev, s_prev], axis=1) * c_aug[hd]
                         + jnp.concatenate([s_loc, s_loc], axis=1) * c_loc[c, hd])
    for hd in heads:
        caug_scr[hd] = c_aug[hd]
    hh, ssum = {}, {}
    for c, hd in pairs:
        e_inter_b = lane_bcast(c, _P_EINTER, hd)
        num = e_inter_b * qc[c, hd][:, 0:LANES] + pv[c, hd][:, 0:LANES]
        den = e_inter_b * qc[c, hd][:, LANES:2 * LANES] + pv[c, hd][:, LANES:2 * LANES]
        hh[c, hd] = num / jnp.maximum(jnp.abs(den), lane_bcast(c, _P_ENM, hd))
        sq = jnp.concatenate(_split2(hh[c, hd] * hh[c, hd]), axis=1).astype(BF16)
        ssum[c, hd] = _dot(sq, ones_ref[...])
    for c, hd in pairs:
        hn = (hh[c, hd] * lax.rsqrt(ssum[c, hd] * np.float32(1.0 / HEAD_DIM) + EPS)
              * mhg_ref[:, cols[hd]])
        mix_scr[rsl[c], cols[hd]] = (hn * so_scr[rsl[c], cols[hd]]).astype(BF16)

    mix = _dot(mix_scr[...], wout_ref[...].astype(BF16))
    o_ref[...] = h_ref[...] + _rms(mix, g3_ref[...])


def _mixer(h, batch, seq, g2, g3, w_in, conv_w, conv_b, b_if, mh_norm_g,
           gmlp_ln_g, gmlp_ln_b, w_spatial, b_spatial, w_out):
    mw = MLSTM_WIDTH
    n_j = seq // MIX_ROWS
    gate0 = 4 * mw
    w_b = w_in[:, gate0 + 2 * HEADS:].astype(BF16)
    w_g = jnp.zeros((D_MODEL, LANES), F32).at[:, 0:2 * HEADS].set(
        w_in[:, gate0:gate0 + 2 * HEADS]).astype(BF16)
    bias = jnp.broadcast_to(b_if.reshape(2 * HEADS, 1), (2 * HEADS, CHUNK))
    ut, ones = _mixer_constants()
    row_spec = pl.BlockSpec((MIX_ROWS, D_MODEL), lambda b, j: (b * n_j + j, 0))
    args = (h, g2.reshape(1, D_MODEL), g3.reshape(1, D_MODEL), w_in, w_b, w_g, bias,
            conv_w, conv_b.reshape(1, 2 * mw), mh_norm_g.reshape(1, mw),
            gmlp_ln_g.reshape(1, GMLP_WIDTH), gmlp_ln_b.reshape(1, GMLP_WIDTH),
            w_spatial, b_spatial.T, w_out, ut, ones)
    in_specs = [row_spec] + [_const_spec(x.shape) for x in args[1:]]
    in_specs[3] = _const_spec((D_MODEL, gate0))
    return pl.pallas_call(
        _mixer_kernel,
        grid=(batch, n_j),
        in_specs=in_specs,
        out_specs=row_spec,
        out_shape=jax.ShapeDtypeStruct(h.shape, F32),
        scratch_shapes=[
            pltpu.VMEM((SUBLANES, 2 * mw), F32),
            pltpu.VMEM((HEADS, HEAD_DIM, 2 * LANES), F32),
            pltpu.VMEM((SUBLANES, CHUNK), F32),
            pltpu.VMEM((MIX_ROWS, mw), BF16),
            pltpu.VMEM((MIX_ROWS, mw), F32),
            pltpu.VMEM((MIX_ROWS, mw), BF16),
            pltpu.VMEM((MIX_ROWS, mw), F32),
            pltpu.VMEM((MIX_ROWS, GMLP_WIDTH), BF16),
            pltpu.VMEM((MIX_ROWS, GMLP_WIDTH), F32),
            pltpu.VMEM((MIX_ROWS, mw + GMLP_WIDTH), BF16),
        ],
        compiler_params=pltpu.CompilerParams(
            dimension_semantics=("arbitrary", "arbitrary"),
            vmem_limit_bytes=VMEM_LIMIT),
        name="mixer",
    )(*args)


def kernel(x, p, ffn1_gu, ffn1_down, ffn2_gu, ffn2_down, w_in, conv_w, conv_b, b_if,
           mh_norm_g, gmlp_ln_g, gmlp_ln_b, w_spatial, b_spatial, w_out, w_ple,
           w_ple_gate, norm_g):
    batch, seq, _ = x.shape
    depth = p.shape[0]
    h = x.reshape(batch * seq, D_MODEL)
    for i in range(depth):
        g = norm_g[i]
        h = _ffn(h, g[0], g[1], ffn1_gu[i], ffn1_down[i])
        h = _mixer(h, batch, seq, g[2], g[3], w_in[i], conv_w[i], conv_b[i], b_if[i],
                   mh_norm_g[i], gmlp_ln_g[i], gmlp_ln_b[i], w_spatial[i], b_spatial[i],
                   w_out[i])
        h = _ffn(h, g[4], g[5], ffn2_gu[i], ffn2_down[i],
                 ple=(p[i].reshape(batch * seq, D_PLE), g[6], g[7], w_ple_gate[i], w_ple[i]))
    return h.reshape(batch, seq, D_MODEL)
```

```python
import functools

import jax
import jax.numpy as jnp
import numpy as np
from jax import lax
from jax.experimental import pallas as pl
from jax.experimental.pallas import tpu as pltpu

D_MODEL = 1024
D_PLE = 256
D_FF = 2816
HEADS = 4
HEAD_DIM = 128
MLSTM_WIDTH = HEADS * HEAD_DIM
GMLP_WIDTH = HEADS * HEAD_DIM
CHUNK = 128
CONV_WIDTH = 4
EPS = 1e-6

LANES = 128
SUBLANES = 8

FFN_ROWS = 512
SLAB = 128
FFN_FC = 256
MIX_ROWS = 512
MIX_SEQS = 2
VMEM_LIMIT = 56 * 1024 * 1024

F32 = jnp.float32
BF16 = jnp.bfloat16


def _rms(x, g):
    return x * lax.rsqrt(jnp.mean(x * x, axis=-1, keepdims=True) + EPS) * g


def _sigmoid(x):
    return 1.0 / (1.0 + jnp.exp(-x))


def _gelu_exact(x):
    return 0.5 * x * (1.0 + lax.erf(x * np.float32(np.sqrt(0.5))))


def _log_sigmoid(x):
    return -(jnp.maximum(-x, 0.0) + jnp.log1p(jnp.exp(-jnp.abs(x))))


def _dot(a, b):
    return jnp.dot(a, b, preferred_element_type=F32)


def _const_spec(shape):
    nd = len(shape)
    return pl.BlockSpec(shape, lambda *_: (0,) * nd, pipeline_mode=pl.Buffered(1))


def _ffn_kernel(*refs, with_ple):
    if with_ple:
        (x_ref, gin_ref, gout_ref, gu_ref, dn_ref, p_ref, g6_ref, g7_ref, wg_ref, wp_ref,
         o_ref, a_scr, gua_scr, gub_scr, acc_scr) = refs
    else:
        x_ref, gin_ref, gout_ref, gu_ref, dn_ref, o_ref, a_scr, gua_scr, gub_scr, acc_scr = refs
    n_f = dn_ref.shape[0]
    fc = dn_ref.shape[1]
    rows = x_ref.shape[0]
    bufs = (gua_scr, gub_scr)
    whole = [slice(0, rows)]
    slabs = [slice(r, r + SLAB) for r in range(0, rows, SLAB)]

    def project(f, gu_scr, row_sets):
        gate_w = gu_ref[:, f * fc:(f + 1) * fc].astype(BF16)
        up_w = gu_ref[:, D_FF + f * fc:D_FF + (f + 1) * fc].astype(BF16)
        for rs in row_sets:
            gu_scr[rs, 0:fc] = _dot(a_scr[rs, :], gate_w)
            gu_scr[rs, fc:2 * fc] = _dot(a_scr[rs, :], up_w)

    def down(f, gu_scr, rs):
        g = gu_scr[rs, 0:fc]
        u = gu_scr[rs, fc:2 * fc]
        act = (g * _sigmoid(g)) * u
        return _dot(act.astype(BF16), dn_ref[f].astype(BF16))

    a_scr[...] = _rms(x_ref[...], gin_ref[...]).astype(BF16)
    project(0, bufs[0], slabs)
    for f in range(n_f - 1):
        project(f + 1, bufs[(f + 1) % 2], whole)
        if f == 0:
            acc_scr[...] = down(f, bufs[f % 2], whole[0])
        else:
            acc_scr[...] += down(f, bufs[f % 2], whole[0])
    hs = []
    for rs in slabs:
        acc = acc_scr[rs, :] + down(n_f - 1, bufs[(n_f - 1) % 2], rs)
        hs.append(x_ref[rs, :] + 0.5 * _rms(acc, gout_ref[...]))
    if with_ple:
        embs = [_dot(p_ref[rs, :].astype(BF16), wp_ref[...]) for rs in slabs]
        gates = [_sigmoid(_dot(_rms(h, g6_ref[...]).astype(BF16), wg_ref[...])) for h in hs]
        hs = [h + _rms(gate * emb, g7_ref[...]) for h, gate, emb in zip(hs, gates, embs)]
    for rs, h in zip(slabs, hs):
        o_ref[rs, :] = h


def _ffn(h, g_in, g_out, w_gu, w_down, ple=None):
    m = h.shape[0]
    n_f = D_FF // FFN_FC
    gu = w_gu
    dn = w_down.reshape(n_f, FFN_FC, D_MODEL)
    row_spec = pl.BlockSpec((FFN_ROWS, D_MODEL), lambda i: (i, 0))
    vec_spec = _const_spec((1, D_MODEL))
    args = [h, g_in.reshape(1, D_MODEL), g_out.reshape(1, D_MODEL), gu, dn]
    in_specs = [row_spec, vec_spec, vec_spec, _const_spec(gu.shape), _const_spec(dn.shape)]
    if ple is not None:
        p, g6, g7, w_gate, w_ple = ple
        args += [p, g6.reshape(1, D_MODEL), g7.reshape(1, D_MODEL),
                 w_gate.astype(BF16), w_ple.astype(BF16)]
        in_specs += [pl.BlockSpec((FFN_ROWS, D_PLE), lambda i: (i, 0)), vec_spec, vec_spec,
                     _const_spec(w_gate.shape), _const_spec(w_ple.shape)]
    return pl.pallas_call(
        functools.partial(_ffn_kernel, with_ple=ple is not None),
        grid=(m // FFN_ROWS,),
        in_specs=in_specs,
        out_specs=row_spec,
        out_shape=jax.ShapeDtypeStruct(h.shape, F32),
        scratch_shapes=[pltpu.VMEM((FFN_ROWS, D_MODEL), BF16),
                        pltpu.VMEM((FFN_ROWS, 2 * FFN_FC), F32),
                        pltpu.VMEM((FFN_ROWS, 2 * FFN_FC), F32),
                        pltpu.VMEM((FFN_ROWS, D_MODEL), F32)],
        compiler_params=pltpu.CompilerParams(
            dimension_semantics=("arbitrary",), vmem_limit_bytes=VMEM_LIMIT),
        name="ffn_ple" if ple is not None else "ffn",
    )(*args)


_P_RT = (0, 1, 2)
_P_EINTER = (3, 4)
_P_ENM = (5, 6)
_P_BLOCKS = 7


def _split2(x):
    hi = x.astype(BF16).astype(F32)
    lo = (x - hi).astype(BF16).astype(F32)
    return hi, lo


def _split3(x):
    hi = x.astype(BF16).astype(F32)
    r1 = x - hi
    mid = r1.astype(BF16).astype(F32)
    lo = (r1 - mid).astype(BF16).astype(F32)
    return hi, mid, lo


def _mixer_constants():
    s_in = np.arange(CHUNK)[:, None]
    s_out = np.arange(CHUNK)[None, :]
    ut = np.concatenate([(s_in <= s_out).astype(np.float32),
                         np.ones((CHUNK, CHUNK), np.float32)], axis=1)
    sel = np.zeros((HEADS, LANES, 3 * LANES), np.float32)
    for hd in range(HEADS):
        for third, blocks in enumerate((_P_RT, _P_EINTER, _P_ENM)):
            for blk in blocks:
                sel[hd, blk * SUBLANES + hd, third * LANES:(third + 1) * LANES] = 1.0
    ones = np.ones((2 * LANES, LANES), np.float32)
    return jnp.asarray(ut, BF16), jnp.asarray(sel, BF16), jnp.asarray(ones, BF16)


def _interleaver(closures):
    pending = list(closures)

    def fill():
        if pending:
            pending.pop(0)()

    def flush():
        while pending:
            pending.pop(0)()

    return fill, flush


def _mixer_kernel(h_ref, g2_ref, g3_ref, wa_ref, wb_ref, wg_ref, bias_ref,
                  convw_ref, convb_ref, mhg_ref, lng_ref, lnb_ref, ws_ref, bs_ref,
                  wout_ref, ut_ref, sel_ref, ones_ref, o_ref,
                  tail_scr, caug_scr, m_scr,
                  q_scr, kf_scr, v_scr, so_scr, vn_scr, ug_scr, mix_scr):
    n_seq, rows = h_ref.shape[0], h_ref.shape[1]
    n_c = rows // CHUNK
    mw = MLSTM_WIDTH
    nr = SUBLANES * n_c
    piece = 2 * LANES

    @pl.when(pl.program_id(1) == 0)
    def _():
        tail_scr[...] = jnp.zeros_like(tail_scr)
        caug_scr[...] = jnp.zeros_like(caug_scr)
        m_scr[...] = jnp.zeros_like(m_scr)

    r_i = lax.broadcasted_iota(jnp.int32, (CHUNK, CHUNK), 0)
    c_i = lax.broadcasted_iota(jnp.int32, (CHUNK, CHUNK), 1)
    causal = c_i <= r_i
    neg_inf = np.float32(-np.inf)
    ones_blk = jnp.ones((CHUNK, HEAD_DIM), BF16)
    w_sp = [jnp.where(causal, ws_ref[hd], 0.0).astype(BF16) for hd in range(HEADS)]
    b_sp = [jnp.broadcast_to(bs_ref[:, hd:hd + 1], (CHUNK, HEAD_DIM)) for hd in range(HEADS)]
    chunks = range(n_c)
    heads = range(HEADS)
    pairs = [(c, hd) for c in chunks for hd in heads]
    rsl = [slice(c * CHUNK, (c + 1) * CHUNK) for c in chunks]
    cols = [slice(hd * HEAD_DIM, (hd + 1) * HEAD_DIM) for hd in heads]
    state = [dict() for _ in range(n_seq)]

    def projections(t):
        d = state[t]

        def proj(w_ref, c0):
            return _dot(d["a"], w_ref[:, c0:c0 + piece].astype(BF16))

        def norm_and_gates():
            d["a"] = _rms(h_ref[t], g2_ref[...]).astype(BF16)
            d["gcol"] = _dot(d["a"], wg_ref[...])
            d["qk"] = []

        def qk_piece(i):
            def run():
                d["qk"].append(proj(wa_ref, i * piece))
            return run

        def gate_prefix_sums():
            i_rows, f_rows = [], []
            for c in chunks:
                x = d["gcol"][rsl[c], :].T[0:SUBLANES, :] + bias_ref[...]
                i_rows.append(x)
                f_rows.append(pltpu.roll(_log_sigmoid(x), HEADS, axis=0))
            d["log_i"] = jnp.concatenate(i_rows, axis=0)
            log_f = jnp.concatenate(f_rows, axis=0)
            pieces = jnp.concatenate(_split3(log_f), axis=0).astype(BF16)
            r = _dot(pieces, ut_ref[...])
            d["r"] = r[0:nr] + r[nr:2 * nr] + r[2 * nr:3 * nr]

        def conv_and_silu():
            qk_raw = jnp.concatenate(d.pop("qk"), axis=1)
            tail = tail_scr[t]
            row8 = lax.broadcasted_iota(jnp.int32, (SUBLANES, 2 * mw), 0)
            conv = qk_raw * convw_ref[CONV_WIDTH - 1:CONV_WIDTH, :] + convb_ref[...]
            for k in range(1, CONV_WIDTH):
                xr = pltpu.roll(qk_raw, k, axis=0)
                tr = pltpu.roll(tail, k, axis=0)
                first = jnp.where(row8 < k, tr, xr[0:SUBLANES])
                xs = jnp.concatenate([first, xr[SUBLANES:]], axis=0)
                conv = conv + xs * convw_ref[CONV_WIDTH - 1 - k:CONV_WIDTH - k, :]
            tail_scr[t] = qk_raw[rows - SUBLANES:rows]
            qk_act = conv * _sigmoid(conv)
            q_scr[t] = qk_act[:, 0:mw].astype(BF16)
            kf_scr[t] = qk_act[:, mw:2 * mw] * np.float32(HEAD_DIM ** -0.5)

        def v_piece(i):
            def run():
                c0 = 2 * mw + i * piece
                v_scr[t, :, i * piece:(i + 1) * piece] = proj(wa_ref, c0).astype(BF16)
            return run

        def o_piece(i):
            def run():
                c0 = 3 * mw + i * piece
                so_scr[t, :, i * piece:(i + 1) * piece] = _sigmoid(proj(wa_ref, c0))
            return run

        def gate_stabilisers():
            b = d["r"][:, 0:CHUNK]
            a_all = d["r"][:, CHUNK:2 * CHUNK]
            cb = d["log_i"] - b
            lane = lax.broadcasted_iota(jnp.int32, (nr, CHUNK), 1)
            cmax = cb
            for sh in (1, 2, 4, 8, 16, 32, 64):
                cmax = jnp.where(lane >= sh, jnp.maximum(cmax, pltpu.roll(cmax, sh, axis=1)), cmax)
            cmax_last = jnp.broadcast_to(cmax[:, CHUNK - 1:CHUNK], (nr, CHUNK))
            m_loc = a_all + cmax_last
            d["cb"] = cb
            d["e_state"] = jnp.exp(cb - cmax_last)
            m = m_scr[t]
            m_ins, d["s_prev"], d["s_loc"] = [], [], []
            for c in chunks:
                sl = slice(SUBLANES * c, SUBLANES * (c + 1))
                m_ins.append(m)
                m_new = jnp.maximum(a_all[sl] + m, m_loc[sl])
                d["s_prev"].append(jnp.exp(a_all[sl] + m - m_new))
                d["s_loc"].append(jnp.exp(m_loc[sl] - m_new))
                m = m_new
            m_scr[t] = m
            inter = b + jnp.concatenate(m_ins, axis=0)
            m_t = jnp.maximum(inter, b + cmax)
            row_blocks = (*_split3(b - m_t), *_split2(jnp.exp(inter - m_t)),
                          *_split2(jnp.exp(-m_t)))
            pad = jnp.zeros((LANES - _P_BLOCKS * SUBLANES, CHUNK), F32)
            d["p_col"] = []
            for c in chunks:
                sl = slice(SUBLANES * c, SUBLANES * (c + 1))
                p_row = jnp.concatenate([blk[sl] for blk in row_blocks] + [pad], axis=0)
                d["p_col"].append(p_row.T.astype(BF16))

        def u_piece(i):
            def run():
                ug_scr[t, :, i * piece:(i + 1) * piece] = _gelu_exact(proj(wb_ref, i * piece))
            return run

        def vg_piece(i):
            def run():
                d.setdefault("vg", []).append(_gelu_exact(proj(wb_ref, GMLP_WIDTH + i * piece)))
            return run

        def layer_norm_v():
            vg = jnp.concatenate(d.pop("vg"), axis=1)
            mu = jnp.mean(vg, axis=-1, keepdims=True)
            var = jnp.mean(jnp.square(vg - mu), axis=-1, keepdims=True)
            vn_scr[t] = ((vg - mu) * lax.rsqrt(var + EPS) * lng_ref[...] + lnb_ref[...]).astype(BF16)

        n_qk, n_v = 2 * mw // piece, mw // piece
        return ([norm_and_gates, qk_piece(0), gate_prefix_sums]
                + [qk_piece(i) for i in range(1, n_qk)] + [conv_and_silu]
                + [v_piece(i) for i in range(n_v)] + [gate_stabilisers]
                + [o_piece(i) for i in range(n_v)]
                + [u_piece(i) for i in range(GMLP_WIDTH // piece)]
                + [vg_piece(i) for i in range(GMLP_WIDTH // piece)] + [layer_norm_v])

    def chunk_stages(t, fill):
        d = state[t]
        group = 2 * HEADS

        def staged(fn):
            out = {}
            for i, (c, hd) in enumerate(pairs):
                out[c, hd] = fn(c, hd)
                if (i + 1) % group == 0:
                    fill()
            return out

        bc = staged(lambda c, hd: _dot(d["p_col"][c], sel_ref[hd]))
        qh = {(c, hd): q_scr[t, rsl[c], cols[hd]] for c, hd in pairs}
        v_aug = {(c, hd): jnp.concatenate([v_scr[t, rsl[c], cols[hd]], ones_blk], axis=1)
                 for c, hd in pairs}
        k_t = {(c, hd): kf_scr[t, rsl[c], cols[hd]].T for c, hd in pairs}
        s = staged(lambda c, hd: _dot(qh[c, hd], k_t[c, hd].astype(BF16)))

        def gmlp(c, hd):
            sv = _dot(w_sp[hd], vn_scr[t, rsl[c], cols[hd]]) + b_sp[hd]
            mix_scr[t, rsl[c], mw + hd * HEAD_DIM:mw + (hd + 1) * HEAD_DIM] = (
                ug_scr[t, rsl[c], cols[hd]] * sv).astype(BF16)
        staged(gmlp)

        def intra(c, hd):
            cb_row = d["cb"][SUBLANES * c + hd:SUBLANES * c + hd + 1, :]
            e_intra = jnp.exp(jnp.where(causal, bc[c, hd][:, 0:LANES] + cb_row, neg_inf))
            qk = (s[c, hd] * e_intra).astype(BF16)
            return _dot(qk, v_aug[c, hd])
        pv = staged(intra)

        def local_state(c, hd):
            es_row = d["e_state"][SUBLANES * c + hd:SUBLANES * c + hd + 1, :]
            return _dot((k_t[c, hd] * es_row).astype(BF16), v_aug[c, hd])
        c_loc = staged(local_state)

        c_aug = [caug_scr[t, hd] for hd in heads]

        def inter_chunk(c, hd):
            qc = _dot(qh[c, hd], c_aug[hd].astype(BF16))
            s_prev = d["s_prev"][c][hd:hd + 1, :]
            s_loc = d["s_loc"][c][hd:hd + 1, :]
            c_aug[hd] = (jnp.concatenate([s_prev, s_prev], axis=1) * c_aug[hd]
                         + jnp.concatenate([s_loc, s_loc], axis=1) * c_loc[c, hd])
            return qc
        qc = staged(inter_chunk)
        for hd in heads:
            caug_scr[t, hd] = c_aug[hd]

        hh = {}

        def normaliser(c, hd):
            e_inter_b = bc[c, hd][:, LANES:2 * LANES]
            num = e_inter_b * qc[c, hd][:, 0:LANES] + pv[c, hd][:, 0:LANES]
            den = e_inter_b * qc[c, hd][:, LANES:2 * LANES] + pv[c, hd][:, LANES:2 * LANES]
            hh[c, hd] = num / jnp.maximum(jnp.abs(den), bc[c, hd][:, 2 * LANES:3 * LANES])
            sq = jnp.concatenate(_split2(hh[c, hd] * hh[c, hd]), axis=1).astype(BF16)
            return _dot(sq, ones_ref[...])
        ssum = staged(normaliser)
        for c, hd in pairs:
            hn = (hh[c, hd] * lax.rsqrt(ssum[c, hd] * np.float32(1.0 / HEAD_DIM) + EPS)
                  * mhg_ref[:, cols[hd]])
            mix_scr[t, rsl[c], cols[hd]] = (hn * so_scr[t, rsl[c], cols[hd]]).astype(BF16)

    def output_projection(t):
        d = state[t]

        def out_piece(i):
            def run():
                d.setdefault("mix", []).append(
                    _dot(mix_scr[t], wout_ref[:, i * piece:(i + 1) * piece].astype(BF16)))
            return run

        def residual():
            mix = jnp.concatenate(d.pop("mix"), axis=1)
            o_ref[t] = h_ref[t] + _rms(mix, g3_ref[...])

        return [out_piece(i) for i in range(D_MODEL // piece)] + [residual]

    for f in projections(0):
        f()
    for t in range(n_seq):
        filler = projections(t + 1) if t + 1 < n_seq else []
        if t > 0:
            filler = output_projection(t - 1) + filler
        fill, flush = _interleaver(filler)
        chunk_stages(t, fill)
        flush()
    for f in output_projection(n_seq - 1):
        f()


def _mixer(h, batch, seq, g2, g3, w_in, conv_w, conv_b, b_if, mh_norm_g,
           gmlp_ln_g, gmlp_ln_b, w_spatial, b_spatial, w_out):
    mw = MLSTM_WIDTH
    n_j = seq // MIX_ROWS
    gate0 = 4 * mw
    w_b = w_in[:, gate0 + 2 * HEADS:].astype(BF16)
    w_g = jnp.zeros((D_MODEL, LANES), F32).at[:, 0:2 * HEADS].set(
        w_in[:, gate0:gate0 + 2 * HEADS]).astype(BF16)
    bias = jnp.broadcast_to(b_if.reshape(2 * HEADS, 1), (2 * HEADS, CHUNK))
    ut, sel, ones = _mixer_constants()
    h4 = h.reshape(batch // MIX_SEQS, MIX_SEQS, seq, D_MODEL)
    tile_spec = pl.BlockSpec((None, MIX_SEQS, MIX_ROWS, D_MODEL), lambda b, j: (b, 0, j, 0))
    args = (h4, g2.reshape(1, D_MODEL), g3.reshape(1, D_MODEL), w_in, w_b, w_g, bias,
            conv_w, conv_b.reshape(1, 2 * mw), mh_norm_g.reshape(1, mw),
            gmlp_ln_g.reshape(1, GMLP_WIDTH), gmlp_ln_b.reshape(1, GMLP_WIDTH),
            w_spatial, b_spatial.T, w_out, ut, sel, ones)
    in_specs = [tile_spec] + [_const_spec(x.shape) for x in args[1:]]
    in_specs[3] = _const_spec((D_MODEL, gate0))

    def per_seq(shape, dtype):
        return pltpu.VMEM((MIX_SEQS,) + shape, dtype)

    out = pl.pallas_call(
        _mixer_kernel,
        grid=(batch // MIX_SEQS, n_j),
        in_specs=in_specs,
        out_specs=tile_spec,
        out_shape=jax.ShapeDtypeStruct(h4.shape, F32),
        scratch_shapes=[
            per_seq((SUBLANES, 2 * mw), F32),
            per_seq((HEADS, HEAD_DIM, 2 * LANES), F32),
            per_seq((SUBLANES, CHUNK), F32),
            per_seq((MIX_ROWS, mw), BF16),
            per_seq((MIX_ROWS, mw), F32),
            per_seq((MIX_ROWS, mw), BF16),
            per_seq((MIX_ROWS, mw), F32),
            per_seq((MIX_ROWS, GMLP_WIDTH), BF16),
            per_seq((MIX_ROWS, GMLP_WIDTH), F32),
            per_seq((MIX_ROWS, mw + GMLP_WIDTH), BF16),
        ],
        compiler_params=pltpu.CompilerParams(
            dimension_semantics=("arbitrary", "arbitrary"),
            vmem_limit_bytes=VMEM_LIMIT),
        name="mixer",
    )(*args)
    return out.reshape(batch * seq, D_MODEL)


def kernel(x, p, ffn1_gu, ffn1_down, ffn2_gu, ffn2_down, w_in, conv_w, conv_b, b_if,
           mh_norm_g, gmlp_ln_g, gmlp_ln_b, w_spatial, b_spatial, w_out, w_ple,
           w_ple_gate, norm_g):
    batch, seq, _ = x.shape
    depth = p.shape[0]
    h = x.reshape(batch * seq, D_MODEL)
    for i in range(depth):
        g = norm_g[i]
        h = _ffn(h, g[0], g[1], ffn1_gu[i], ffn1_down[i])
        h = _mixer(h, batch, seq, g[2], g[3], w_in[i], conv_w[i], conv_b[i], b_if[i],
                   mh_norm_g[i], gmlp_ln_g[i], gmlp_ln_b[i], w_spatial[i], b_spatial[i],
                   w_out[i])
        h = _ffn(h, g[4], g[5], ffn2_gu[i], ffn2_down[i],
                 ple=(p[i].reshape(batch * seq, D_PLE), g[6], g[7], w_ple_gate[i], w_ple[i]))
    return h.reshape(batch, seq, D_MODEL)
```

```python
import functools

import jax
import jax.numpy as jnp
import numpy as np
from jax import lax
from jax.experimental import pallas as pl
from jax.experimental.pallas import tpu as pltpu

D_MODEL = 1024
D_PLE = 256
D_FF = 2816
HEADS = 4
HEAD_DIM = 128
MLSTM_WIDTH = HEADS * HEAD_DIM
GMLP_WIDTH = HEADS * HEAD_DIM
CHUNK = 128
CONV_WIDTH = 4
EPS = 1e-6

LANES = 128
SUBLANES = 8

FFN_ROWS = 512
SLAB = 128
FFN_FC = 256
MIX_ROWS = 512
MIX_SEQS = 2
VMEM_LIMIT = 56 * 1024 * 1024

F32 = jnp.float32
BF16 = jnp.bfloat16


def _rms(x, g):
    return x * lax.rsqrt(jnp.mean(x * x, axis=-1, keepdims=True) + EPS) * g


def _sigmoid(x):
    return 1.0 / (1.0 + jnp.exp(-x))


def _gelu_exact(x):
    return 0.5 * x * (1.0 + lax.erf(x * np.float32(np.sqrt(0.5))))


def _log_sigmoid(x):
    return -(jnp.maximum(-x, 0.0) + jnp.log1p(jnp.exp(-jnp.abs(x))))


def _dot(a, b):
    return jnp.dot(a, b, preferred_element_type=F32)


def _const_spec(shape):
    nd = len(shape)
    return pl.BlockSpec(shape, lambda *_: (0,) * nd, pipeline_mode=pl.Buffered(1))


def _ffn_kernel(*refs, with_ple):
    if with_ple:
        (x_ref, gin_ref, gout_ref, gu_ref, dn_ref, p_ref, g6_ref, g7_ref, wg_ref, wp_ref,
         o_ref, a_scr, act_scr) = refs
    else:
        x_ref, gin_ref, gout_ref, gu_ref, dn_ref, o_ref, a_scr, act_scr = refs
    n_f = D_FF // FFN_FC
    rows = x_ref.shape[0]
    slabs = [slice(r, r + SLAB) for r in range(0, rows, SLAB)]

    a_scr[...] = _rms(x_ref[...], gin_ref[...]).astype(BF16)
    for f in range(n_f):
        cs = slice(f * FFN_FC, (f + 1) * FFN_FC)
        gate_w = gu_ref[:, cs].astype(BF16)
        up_w = gu_ref[:, D_FF + f * FFN_FC:D_FF + (f + 1) * FFN_FC].astype(BF16)
        row_sets = slabs if f == 0 else [slice(0, rows)]
        for rs in row_sets:
            g = _dot(a_scr[rs, :], gate_w)
            u = _dot(a_scr[rs, :], up_w)
            act_scr[rs, cs] = ((g * _sigmoid(g)) * u).astype(BF16)
    y = _dot(act_scr[...], dn_ref[...].astype(BF16))
    hs = [x_ref[rs, :] + 0.5 * _rms(y[rs, :], gout_ref[...]) for rs in slabs]
    if with_ple:
        embs = [_dot(p_ref[rs, :].astype(BF16), wp_ref[...]) for rs in slabs]
        gates = [_sigmoid(_dot(_rms(h, g6_ref[...]).astype(BF16), wg_ref[...])) for h in hs]
        hs = [h + _rms(gate * emb, g7_ref[...]) for h, gate, emb in zip(hs, gates, embs)]
    for rs, h in zip(slabs, hs):
        o_ref[rs, :] = h


def _ffn(h, g_in, g_out, w_gu, w_down, ple=None):
    m = h.shape[0]
    gu, dn = w_gu, w_down
    row_spec = pl.BlockSpec((FFN_ROWS, D_MODEL), lambda i: (i, 0))
    vec_spec = _const_spec((1, D_MODEL))
    args = [h, g_in.reshape(1, D_MODEL), g_out.reshape(1, D_MODEL), gu, dn]
    in_specs = [row_spec, vec_spec, vec_spec, _const_spec(gu.shape), _const_spec(dn.shape)]
    if ple is not None:
        p, g6, g7, w_gate, w_ple = ple
        args += [p, g6.reshape(1, D_MODEL), g7.reshape(1, D_MODEL),
                 w_gate.astype(BF16), w_ple.astype(BF16)]
        in_specs += [pl.BlockSpec((FFN_ROWS, D_PLE), lambda i: (i, 0)), vec_spec, vec_spec,
                     _const_spec(w_gate.shape), _const_spec(w_ple.shape)]
    return pl.pallas_call(
        functools.partial(_ffn_kernel, with_ple=ple is not None),
        grid=(m // FFN_ROWS,),
        in_specs=in_specs,
        out_specs=row_spec,
        out_shape=jax.ShapeDtypeStruct(h.shape, F32),
        scratch_shapes=[pltpu.VMEM((FFN_ROWS, D_MODEL), BF16),
                        pltpu.VMEM((FFN_ROWS, D_FF), BF16)],
        compiler_params=pltpu.CompilerParams(
            dimension_semantics=("arbitrary",), vmem_limit_bytes=VMEM_LIMIT),
        name="ffn_ple" if ple is not None else "ffn",
    )(*args)


_P_RT = (0, 1, 2)
_P_EINTER = (3, 4)
_P_ENM = (5, 6)
_P_BLOCKS = 7


def _split2(x):
    hi = x.astype(BF16).astype(F32)
    lo = (x - hi).astype(BF16).astype(F32)
    return hi, lo


def _split3(x):
    hi = x.astype(BF16).astype(F32)
    r1 = x - hi
    mid = r1.astype(BF16).astype(F32)
    lo = (r1 - mid).astype(BF16).astype(F32)
    return hi, mid, lo


def _mixer_constants():
    s_in = np.arange(CHUNK)[:, None]
    s_out = np.arange(CHUNK)[None, :]
    ut = np.concatenate([(s_in <= s_out).astype(np.float32),
                         np.ones((CHUNK, CHUNK), np.float32)], axis=1)
    sel = np.zeros((HEADS, LANES, 3 * LANES), np.float32)
    for hd in range(HEADS):
        for third, blocks in enumerate((_P_RT, _P_EINTER, _P_ENM)):
            for blk in blocks:
                sel[hd, blk * SUBLANES + hd, third * LANES:(third + 1) * LANES] = 1.0
    ones = np.ones((2 * LANES, LANES), np.float32)
    return jnp.asarray(ut, BF16), jnp.asarray(sel, BF16), jnp.asarray(ones, BF16)


def _interleaver(closures):
    pending = list(closures)

    def fill():
        if pending:
            pending.pop(0)()

    def flush():
        while pending:
            pending.pop(0)()

    return fill, flush


def _gmlp_weight_kernel(wmid_ref, wtail_ref, wb_ref, wg_ref):
    mid = wmid_ref[...]
    n_gate = 2 * HEADS
    piece = 2 * LANES
    lane = lax.broadcasted_iota(jnp.int32, (D_MODEL, LANES), 1)
    wg_ref[...] = jnp.where(lane < n_gate, mid[:, 0:LANES], 0.0).astype(BF16)
    width = mid.shape[1]
    mid_b = mid.astype(BF16)
    tail_b = wtail_ref[...].astype(BF16)
    for c0 in range(0, width, piece):
        src = lax.broadcasted_iota(jnp.int32, (width, piece), 0)
        dst = lax.broadcasted_iota(jnp.int32, (width, piece), 1) + c0
        shift = jnp.where(src == dst + n_gate, 1.0, 0.0).astype(BF16)
        src_t = lax.broadcasted_iota(jnp.int32, (LANES, piece), 0)
        dst_t = lax.broadcasted_iota(jnp.int32, (LANES, piece), 1) + c0
        place = jnp.where((dst_t == src_t + (width - n_gate)) & (src_t < n_gate),
                          1.0, 0.0).astype(BF16)
        wb_ref[:, c0:c0 + piece] = (_dot(mid_b, shift) + _dot(tail_b, place)).astype(BF16)


def _gmlp_weights(w_in):
    gate0 = 4 * MLSTM_WIDTH
    n_mid = 2 * GMLP_WIDTH
    w_tail = jnp.pad(w_in[:, gate0 + n_mid:], ((0, 0), (0, LANES - 2 * HEADS)))
    return pl.pallas_call(
        _gmlp_weight_kernel,
        grid=(1,),
        in_specs=[pl.BlockSpec((D_MODEL, n_mid), lambda i: (0, gate0 // n_mid)),
                  pl.BlockSpec((D_MODEL, LANES), lambda i: (0, 0))],
        out_specs=[pl.BlockSpec((D_MODEL, n_mid), lambda i: (0, 0)),
                   pl.BlockSpec((D_MODEL, LANES), lambda i: (0, 0))],
        out_shape=[jax.ShapeDtypeStruct((D_MODEL, n_mid), BF16),
                   jax.ShapeDtypeStruct((D_MODEL, LANES), BF16)],
        compiler_params=pltpu.CompilerParams(dimension_semantics=("arbitrary",)),
        name="gmlp_weights",
    )(w_in, w_tail)


def _mixer_kernel(h_ref, g2_ref, g3_ref, wa_ref, wb_ref, wg_ref, bias_ref,
                  convw_ref, convb_ref, mhg_ref, lng_ref, lnb_ref, ws_ref, bs_ref,
                  wout_ref, ut_ref, sel_ref, ones_ref, o_ref,
                  qk_scr, caug_scr, m_scr,
                  q_scr, kf_scr, v_scr, so_scr, vn_scr, ug_scr, mix_scr):
    n_seq, rows = h_ref.shape[0], h_ref.shape[1]
    n_c = rows // CHUNK
    mw = MLSTM_WIDTH
    nr = SUBLANES * n_c
    piece = 2 * LANES

    @pl.when(pl.program_id(1) == 0)
    def _():
        qk_scr[:, 0:SUBLANES, :] = jnp.zeros((n_seq, SUBLANES, 2 * mw), F32)
        caug_scr[...] = jnp.zeros_like(caug_scr)
        m_scr[...] = jnp.zeros_like(m_scr)

    r_i = lax.broadcasted_iota(jnp.int32, (CHUNK, CHUNK), 0)
    c_i = lax.broadcasted_iota(jnp.int32, (CHUNK, CHUNK), 1)
    causal = c_i <= r_i
    neg_inf = np.float32(-np.inf)
    ones_blk = jnp.ones((CHUNK, HEAD_DIM), BF16)
    w_sp = [jnp.where(causal, ws_ref[hd], 0.0).astype(BF16) for hd in range(HEADS)]
    b_sp = [jnp.broadcast_to(bs_ref[:, hd:hd + 1], (CHUNK, HEAD_DIM)) for hd in range(HEADS)]
    chunks = range(n_c)
    heads = range(HEADS)
    pairs = [(c, hd) for c in chunks for hd in heads]
    rsl = [slice(c * CHUNK, (c + 1) * CHUNK) for c in chunks]
    cols = [slice(hd * HEAD_DIM, (hd + 1) * HEAD_DIM) for hd in heads]
    state = [dict() for _ in range(n_seq)]

    def projections(t):
        d = state[t]

        def proj(w_ref, c0):
            return _dot(d["a"], w_ref[:, c0:c0 + piece].astype(BF16))

        def norm_and_gates():
            d["a"] = _rms(h_ref[t], g2_ref[...]).astype(BF16)
            d["gcol"] = _dot(d["a"], wg_ref[...])

        def qk_piece(i):
            def run():
                qk_scr[t, SUBLANES:SUBLANES + rows, i * piece:(i + 1) * piece] = proj(wa_ref, i * piece)
            return run

        def gate_prefix_sums():
            i_rows, f_rows = [], []
            for c in chunks:
                x = d["gcol"][rsl[c], :].T[0:SUBLANES, :] + bias_ref[...]
                i_rows.append(x)
                f_rows.append(pltpu.roll(_log_sigmoid(x), HEADS, axis=0))
            d["log_i"] = jnp.concatenate(i_rows, axis=0)
            log_f = jnp.concatenate(f_rows, axis=0)
            pieces = jnp.concatenate(_split3(log_f), axis=0).astype(BF16)
            r = _dot(pieces, ut_ref[...])
            d["r"] = r[0:nr] + r[nr:2 * nr] + r[2 * nr:3 * nr]

        def conv_and_silu():
            conv = convb_ref[...]
            for k in range(CONV_WIDTH):
                x_k = qk_scr[t, SUBLANES - k:SUBLANES - k + rows, :]
                conv = conv + x_k * convw_ref[CONV_WIDTH - 1 - k:CONV_WIDTH - k, :]
            qk_scr[t, 0:SUBLANES, :] = qk_scr[t, rows:rows + SUBLANES, :]
            qk_act = conv * _sigmoid(conv)
            q_scr[t] = qk_act[:, 0:mw].astype(BF16)
            kf_scr[t] = qk_act[:, mw:2 * mw] * np.float32(HEAD_DIM ** -0.5)

        def v_piece(i):
            def run():
                c0 = 2 * mw + i * piece
                v_scr[t, :, i * piece:(i + 1) * piece] = proj(wa_ref, c0).astype(BF16)
            return run

        def o_piece(i):
            def run():
                c0 = 3 * mw + i * piece
                so_scr[t, :, i * piece:(i + 1) * piece] = _sigmoid(proj(wa_ref, c0))
            return run

        def gate_stabilisers():
            b = d["r"][:, 0:CHUNK]
            a_all = d["r"][:, CHUNK:2 * CHUNK]
            cb = d["log_i"] - b
            lane = lax.broadcasted_iota(jnp.int32, (nr, CHUNK), 1)
            cmax = cb
            for sh in (1, 2, 4, 8, 16, 32, 64):
                cmax = jnp.where(lane >= sh, jnp.maximum(cmax, pltpu.roll(cmax, sh, axis=1)), cmax)
            cmax_last = jnp.broadcast_to(cmax[:, CHUNK - 1:CHUNK], (nr, CHUNK))
            m_loc = a_all + cmax_last
            d["cb"] = cb
            d["e_state"] = jnp.exp(cb - cmax_last)
            m = m_scr[t]
            m_ins, d["s_prev"], d["s_loc"] = [], [], []
            for c in chunks:
                sl = slice(SUBLANES * c, SUBLANES * (c + 1))
                m_ins.append(m)
                m_new = jnp.maximum(a_all[sl] + m, m_loc[sl])
                d["s_prev"].append(jnp.exp(a_all[sl] + m - m_new))
                d["s_loc"].append(jnp.exp(m_loc[sl] - m_new))
                m = m_new
            m_scr[t] = m
            inter = b + jnp.concatenate(m_ins, axis=0)
            m_t = jnp.maximum(inter, b + cmax)
            row_blocks = (*_split3(b - m_t), *_split2(jnp.exp(inter - m_t)),
                          *_split2(jnp.exp(-m_t)))
            pad = jnp.zeros((LANES - _P_BLOCKS * SUBLANES, CHUNK), F32)
            d["p_col"] = []
            for c in chunks:
                sl = slice(SUBLANES * c, SUBLANES * (c + 1))
                p_row = jnp.concatenate([blk[sl] for blk in row_blocks] + [pad], axis=0)
                d["p_col"].append(p_row.T.astype(BF16))

        def u_piece(i):
            def run():
                ug_scr[t, :, i * piece:(i + 1) * piece] = _gelu_exact(proj(wb_ref, i * piece))
            return run

        def vg_piece(i):
            def run():
                d.setdefault("vg", []).append(_gelu_exact(proj(wb_ref, GMLP_WIDTH + i * piece)))
            return run

        def layer_norm_v():
            vg = jnp.concatenate(d.pop("vg"), axis=1)
            mu = jnp.mean(vg, axis=-1, keepdims=True)
            var = jnp.mean(jnp.square(vg - mu), axis=-1, keepdims=True)
            vn_scr[t] = ((vg - mu) * lax.rsqrt(var + EPS) * lng_ref[...] + lnb_ref[...]).astype(BF16)

        n_qk, n_v = 2 * mw // piece, mw // piece
        return ([norm_and_gates, qk_piece(0), gate_prefix_sums]
                + [qk_piece(i) for i in range(1, n_qk)] + [conv_and_silu]
                + [v_piece(i) for i in range(n_v)] + [gate_stabilisers]
                + [o_piece(i) for i in range(n_v)]
                + [u_piece(i) for i in range(GMLP_WIDTH // piece)]
                + [vg_piece(i) for i in range(GMLP_WIDTH // piece)] + [layer_norm_v])

    def chunk_stages(t, fill):
        d = state[t]
        group = 2 * HEADS

        def staged(fn):
            out = {}
            for i, (c, hd) in enumerate(pairs):
                out[c, hd] = fn(c, hd)
                if (i + 1) % group == 0:
                    fill()
            return out

        bc = staged(lambda c, hd: _dot(d["p_col"][c], sel_ref[hd]))
        qh = {(c, hd): q_scr[t, rsl[c], cols[hd]] for c, hd in pairs}
        v_aug = {(c, hd): jnp.concatenate([v_scr[t, rsl[c], cols[hd]], ones_blk], axis=1)
                 for c, hd in pairs}
        k_t = {(c, hd): kf_scr[t, rsl[c], cols[hd]].T for c, hd in pairs}
        s = staged(lambda c, hd: _dot(qh[c, hd], k_t[c, hd].astype(BF16)))

        def gmlp(c, hd):
            sv = _dot(w_sp[hd], vn_scr[t, rsl[c], cols[hd]]) + b_sp[hd]
            mix_scr[t, rsl[c], mw + hd * HEAD_DIM:mw + (hd + 1) * HEAD_DIM] = (
                ug_scr[t, rsl[c], cols[hd]] * sv).astype(BF16)
        staged(gmlp)

        def intra(c, hd):
            cb_row = d["cb"][SUBLANES * c + hd:SUBLANES * c + hd + 1, :]
            e_intra = jnp.exp(jnp.where(causal, bc[c, hd][:, 0:LANES] + cb_row, neg_inf))
            qk = (s[c, hd] * e_intra).astype(BF16)
            return _dot(qk, v_aug[c, hd])
        pv = staged(intra)

        def local_state(c, hd):
            es_row = d["e_state"][SUBLANES * c + hd:SUBLANES * c + hd + 1, :]
            return _dot((k_t[c, hd] * es_row).astype(BF16), v_aug[c, hd])
        c_loc = staged(local_state)

        c_aug = [caug_scr[t, hd] for hd in heads]

        def inter_chunk(c, hd):
            qc = _dot(qh[c, hd], c_aug[hd].astype(BF16))
            s_prev = d["s_prev"][c][hd:hd + 1, :]
            s_loc = d["s_loc"][c][hd:hd + 1, :]
            c_aug[hd] = (jnp.concatenate([s_prev, s_prev], axis=1) * c_aug[hd]
                         + jnp.concatenate([s_loc, s_loc], axis=1) * c_loc[c, hd])
            return qc
        qc = staged(inter_chunk)
        for hd in heads:
            caug_scr[t, hd] = c_aug[hd]

        hh = {}

        def normaliser(c, hd):
            e_inter_b = bc[c, hd][:, LANES:2 * LANES]
            num = e_inter_b * qc[c, hd][:, 0:LANES] + pv[c, hd][:, 0:LANES]
            den = e_inter_b * qc[c, hd][:, LANES:2 * LANES] + pv[c, hd][:, LANES:2 * LANES]
            hh[c, hd] = num / jnp.maximum(jnp.abs(den), bc[c, hd][:, 2 * LANES:3 * LANES])
            sq = jnp.concatenate(_split2(hh[c, hd] * hh[c, hd]), axis=1).astype(BF16)
            return _dot(sq, ones_ref[...])
        ssum = staged(normaliser)
        for c, hd in pairs:
            hn = (hh[c, hd] * lax.rsqrt(ssum[c, hd] * np.float32(1.0 / HEAD_DIM) + EPS)
                  * mhg_ref[:, cols[hd]])
            mix_scr[t, rsl[c], cols[hd]] = (hn * so_scr[t, rsl[c], cols[hd]]).astype(BF16)

    def output_projection(t):
        d = state[t]

        def out_piece(i):
            def run():
                d.setdefault("mix", []).append(
                    _dot(mix_scr[t], wout_ref[:, i * piece:(i + 1) * piece].astype(BF16)))
            return run

        def residual():
            mix = jnp.concatenate(d.pop("mix"), axis=1)
            o_ref[t] = h_ref[t] + _rms(mix, g3_ref[...])

        return [out_piece(i) for i in range(D_MODEL // piece)] + [residual]

    for f in projections(0):
        f()
    for t in range(n_seq):
        filler = projections(t + 1) if t + 1 < n_seq else []
        if t > 0:
            filler = output_projection(t - 1) + filler
        fill, flush = _interleaver(filler)
        chunk_stages(t, fill)
        flush()
    for f in output_projection(n_seq - 1):
        f()


def _mixer(h, batch, seq, g2, g3, w_in, conv_w, conv_b, b_if, mh_norm_g,
           gmlp_ln_g, gmlp_ln_b, w_spatial, b_spatial, w_out):
    mw = MLSTM_WIDTH
    n_j = seq // MIX_ROWS
    gate0 = 4 * mw
    w_b, w_g = _gmlp_weights(w_in)
    bias = jnp.broadcast_to(b_if.reshape(2 * HEADS, 1), (2 * HEADS, CHUNK))
    ut, sel, ones = _mixer_constants()
    h4 = h.reshape(batch // MIX_SEQS, MIX_SEQS, seq, D_MODEL)
    tile_spec = pl.BlockSpec((None, MIX_SEQS, MIX_ROWS, D_MODEL), lambda b, j: (b, 0, j, 0))
    args = (h4, g2.reshape(1, D_MODEL), g3.reshape(1, D_MODEL), w_in, w_b, w_g, bias,
            conv_w, conv_b.reshape(1, 2 * mw), mh_norm_g.reshape(1, mw),
            gmlp_ln_g.reshape(1, GMLP_WIDTH), gmlp_ln_b.reshape(1, GMLP_WIDTH),
            w_spatial, b_spatial.T, w_out, ut, sel, ones)
    in_specs = [tile_spec] + [_const_spec(x.shape) for x in args[1:]]
    in_specs[3] = _const_spec((D_MODEL, gate0))

    def per_seq(shape, dtype):
        return pltpu.VMEM((MIX_SEQS,) + shape, dtype)

    out = pl.pallas_call(
        _mixer_kernel,
        grid=(batch // MIX_SEQS, n_j),
        in_specs=in_specs,
        out_specs=tile_spec,
        out_shape=jax.ShapeDtypeStruct(h4.shape, F32),
        scratch_shapes=[
            per_seq((SUBLANES + MIX_ROWS, 2 * mw), F32),
            per_seq((HEADS, HEAD_DIM, 2 * LANES), F32),
            per_seq((SUBLANES, CHUNK), F32),
            per_seq((MIX_ROWS, mw), BF16),
            per_seq((MIX_ROWS, mw), F32),
            per_seq((MIX_ROWS, mw), BF16),
            per_seq((MIX_ROWS, mw), F32),
            per_seq((MIX_ROWS, GMLP_WIDTH), BF16),
            per_seq((MIX_ROWS, GMLP_WIDTH), F32),
            per_seq((MIX_ROWS, mw + GMLP_WIDTH), BF16),
        ],
        compiler_params=pltpu.CompilerParams(
            dimension_semantics=("arbitrary", "arbitrary"),
            vmem_limit_bytes=VMEM_LIMIT),
        name="mixer",
    )(*args)
    return out.reshape(batch * seq, D_MODEL)


def kernel(x, p, ffn1_gu, ffn1_down, ffn2_gu, ffn2_down, w_in, conv_w, conv_b, b_if,
           mh_norm_g, gmlp_ln_g, gmlp_ln_b, w_spatial, b_spatial, w_out, w_ple,
           w_ple_gate, norm_g):
    batch, seq, _ = x.shape
    depth = p.shape[0]
    h = x.reshape(batch * seq, D_MODEL)
    for i in range(depth):
        g = norm_g[i]
        h = _ffn(h, g[0], g[1], ffn1_gu[i], ffn1_down[i])
        h = _mixer(h, batch, seq, g[2], g[3], w_in[i], conv_w[i], conv_b[i], b_if[i],
                   mh_norm_g[i], gmlp_ln_g[i], gmlp_ln_b[i], w_spatial[i], b_spatial[i],
                   w_out[i])
        h = _ffn(h, g[4], g[5], ffn2_gu[i], ffn2_down[i],
                 ple=(p[i].reshape(batch * seq, D_PLE), g[6], g[7], w_ple_gate[i], w_ple[i]))
    return h.reshape(batch, seq, D_MODEL)
```

```python
import functools

import jax
import jax.numpy as jnp
import numpy as np
from jax import lax
from jax.experimental import pallas as pl
from jax.experimental.pallas import tpu as pltpu

D_MODEL = 1024
D_PLE = 256
D_FF = 2816
HEADS = 4
HEAD_DIM = 128
MLSTM_WIDTH = HEADS * HEAD_DIM
GMLP_WIDTH = HEADS * HEAD_DIM
CHUNK = 128
CONV_WIDTH = 4
EPS = 1e-6

LANES = 128
SUBLANES = 8

FFN_ROWS = 512
HEAD_SLAB = 128
TAIL_SLAB = 256
FFN_FC = 256
MIX_ROWS = 512
MIX_SEQS = 2
VMEM_LIMIT = 56 * 1024 * 1024

F32 = jnp.float32
BF16 = jnp.bfloat16


def _rms(x, g):
    return x * lax.rsqrt(jnp.mean(x * x, axis=-1, keepdims=True) + EPS) * g


def _sigmoid(x):
    return 1.0 / (1.0 + jnp.exp(-x))


def _gelu_exact(x):
    return 0.5 * x * (1.0 + lax.erf(x * np.float32(np.sqrt(0.5))))


def _log_sigmoid(x):
    return -(jnp.maximum(-x, 0.0) + jnp.log1p(jnp.exp(-jnp.abs(x))))


def _dot(a, b):
    return jnp.dot(a, b, preferred_element_type=F32)


def _const_spec(shape):
    nd = len(shape)
    return pl.BlockSpec(shape, lambda *_: (0,) * nd, pipeline_mode=pl.Buffered(1))


def _ffn_kernel(*refs, with_ple):
    if with_ple:
        (x_ref, gin_ref, gout_ref, gu_ref, dn_ref, p_ref, g6_ref, g7_ref, wg_ref, wp_ref,
         o_ref, a_scr, act_scr) = refs
    else:
        x_ref, gin_ref, gout_ref, gu_ref, dn_ref, o_ref, a_scr, act_scr = refs
    n_f = D_FF // FFN_FC
    rows = x_ref.shape[0]
    head_slabs = [slice(r, r + HEAD_SLAB) for r in range(0, rows, HEAD_SLAB)]
    slabs = [slice(r, r + TAIL_SLAB) for r in range(0, rows, TAIL_SLAB)]

    a_scr[...] = _rms(x_ref[...], gin_ref[...]).astype(BF16)
    for f in range(n_f):
        cs = slice(f * FFN_FC, (f + 1) * FFN_FC)
        gate_w = gu_ref[:, cs].astype(BF16)
        up_w = gu_ref[:, D_FF + f * FFN_FC:D_FF + (f + 1) * FFN_FC].astype(BF16)
        row_sets = head_slabs if f == 0 else [slice(0, rows)]
        for rs in row_sets:
            g = _dot(a_scr[rs, :], gate_w)
            u = _dot(a_scr[rs, :], up_w)
            act_scr[rs, cs] = ((g * _sigmoid(g)) * u).astype(BF16)
    y = _dot(act_scr[...], dn_ref[...].astype(BF16))
    hs = [x_ref[rs, :] + 0.5 * _rms(y[rs, :], gout_ref[...]) for rs in slabs]
    if with_ple:
        embs = [_dot(p_ref[rs, :].astype(BF16), wp_ref[...]) for rs in slabs]
        gates = [_sigmoid(_dot(_rms(h, g6_ref[...]).astype(BF16), wg_ref[...])) for h in hs]
        hs = [h + _rms(gate * emb, g7_ref[...]) for h, gate, emb in zip(hs, gates, embs)]
    for rs, h in zip(slabs, hs):
        o_ref[rs, :] = h


def _ffn(h, g_in, g_out, w_gu, w_down, ple=None):
    m = h.shape[0]
    gu, dn = w_gu, w_down
    row_spec = pl.BlockSpec((FFN_ROWS, D_MODEL), lambda i: (i, 0))
    vec_spec = _const_spec((1, D_MODEL))
    args = [h, g_in.reshape(1, D_MODEL), g_out.reshape(1, D_MODEL), gu, dn]
    in_specs = [row_spec, vec_spec, vec_spec, _const_spec(gu.shape), _const_spec(dn.shape)]
    if ple is not None:
        p, g6, g7, w_gate, w_ple = ple
        args += [p, g6.reshape(1, D_MODEL), g7.reshape(1, D_MODEL),
                 w_gate.astype(BF16), w_ple.astype(BF16)]
        in_specs += [pl.BlockSpec((FFN_ROWS, D_PLE), lambda i: (i, 0)), vec_spec, vec_spec,
                     _const_spec(w_gate.shape), _const_spec(w_ple.shape)]
    return pl.pallas_call(
        functools.partial(_ffn_kernel, with_ple=ple is not None),
        grid=(m // FFN_ROWS,),
        in_specs=in_specs,
        out_specs=row_spec,
        out_shape=jax.ShapeDtypeStruct(h.shape, F32),
        scratch_shapes=[pltpu.VMEM((FFN_ROWS, D_MODEL), BF16),
                        pltpu.VMEM((FFN_ROWS, D_FF), BF16)],
        compiler_params=pltpu.CompilerParams(
            dimension_semantics=("arbitrary",), vmem_limit_bytes=VMEM_LIMIT),
        name="ffn_ple" if ple is not None else "ffn",
    )(*args)


_P_RT = (0, 1, 2)
_P_EINTER = (3, 4)
_P_ENM = (5, 6)
_P_BLOCKS = 7


def _split2(x):
    hi = x.astype(BF16).astype(F32)
    lo = (x - hi).astype(BF16).astype(F32)
    return hi, lo


def _split3(x):
    hi = x.astype(BF16).astype(F32)
    r1 = x - hi
    mid = r1.astype(BF16).astype(F32)
    lo = (r1 - mid).astype(BF16).astype(F32)
    return hi, mid, lo


def _mixer_constants():
    s_in = np.arange(CHUNK)[:, None]
    s_out = np.arange(CHUNK)[None, :]
    ut = np.concatenate([(s_in <= s_out).astype(np.float32),
                         np.ones((CHUNK, CHUNK), np.float32)], axis=1)
    sel = np.zeros((HEADS, LANES, 3 * LANES), np.float32)
    for hd in range(HEADS):
        for third, blocks in enumerate((_P_RT, _P_EINTER, _P_ENM)):
            for blk in blocks:
                sel[hd, blk * SUBLANES + hd, third * LANES:(third + 1) * LANES] = 1.0
    ones = np.ones((2 * LANES, LANES), np.float32)
    return jnp.asarray(ut, BF16), jnp.asarray(sel, BF16), jnp.asarray(ones, BF16)


def _interleaver(closures):
    pending = list(closures)

    def fill():
        if pending:
            pending.pop(0)()

    def flush():
        while pending:
            pending.pop(0)()

    return fill, flush


def _mixer_weight_kernel(wt_ref, wa_ref, wb_ref, wg_ref):
    piece = 2 * LANES
    gate0 = wa_ref.shape[1]
    n_gate = 2 * HEADS
    for c0 in range(0, gate0, piece):
        wa_ref[:, c0:c0 + piece] = wt_ref[c0:c0 + piece, :].T.astype(BF16)
    lane = lax.broadcasted_iota(jnp.int32, (D_MODEL, LANES), 1)
    wg_ref[...] = jnp.where(lane < n_gate, wt_ref[gate0:gate0 + LANES, :].T, 0.0).astype(BF16)
    for c0 in range(0, wb_ref.shape[1], piece):
        r0 = gate0 + n_gate + c0
        wb_ref[:, c0:c0 + piece] = wt_ref[r0:r0 + piece, :].T.astype(BF16)


def _mixer_weights(w_in):
    gate0 = 4 * MLSTM_WIDTH
    n_b = 2 * GMLP_WIDTH
    w_t = jnp.swapaxes(w_in, 0, 1)
    full = lambda shape: pl.BlockSpec(shape, lambda i: (0, 0))
    return pl.pallas_call(
        _mixer_weight_kernel,
        grid=(1,),
        in_specs=[full(w_t.shape)],
        out_specs=[full((D_MODEL, gate0)), full((D_MODEL, n_b)), full((D_MODEL, LANES))],
        out_shape=[jax.ShapeDtypeStruct((D_MODEL, gate0), BF16),
                   jax.ShapeDtypeStruct((D_MODEL, n_b), BF16),
                   jax.ShapeDtypeStruct((D_MODEL, LANES), BF16)],
        compiler_params=pltpu.CompilerParams(
            dimension_semantics=("arbitrary",), vmem_limit_bytes=VMEM_LIMIT),
        name="mixer_weights",
    )(w_t)


def _mixer_kernel(h_ref, g2_ref, g3_ref, wa_ref, wb_ref, wg_ref, bias_ref,
                  convw_ref, convb_ref, mhg_ref, lng_ref, lnb_ref, ws_ref, bs_ref,
                  wout_ref, ut_ref, sel_ref, ones_ref, o_ref,
                  qk_scr, caug_scr, m_scr,
                  q_scr, kf_scr, v_scr, so_scr, vn_scr, ug_scr, mix_scr):
    n_seq, rows = h_ref.shape[0], h_ref.shape[1]
    n_c = rows // CHUNK
    mw = MLSTM_WIDTH
    nr = SUBLANES * n_c
    piece = 2 * LANES

    @pl.when(pl.program_id(1) == 0)
    def _():
        qk_scr[:, 0:SUBLANES, :] = jnp.zeros((n_seq, SUBLANES, 2 * mw), F32)
        caug_scr[...] = jnp.zeros_like(caug_scr)
        m_scr[...] = jnp.zeros_like(m_scr)

    r_i = lax.broadcasted_iota(jnp.int32, (CHUNK, CHUNK), 0)
    c_i = lax.broadcasted_iota(jnp.int32, (CHUNK, CHUNK), 1)
    causal = c_i <= r_i
    neg_inf = np.float32(-np.inf)
    ones_blk = jnp.ones((CHUNK, HEAD_DIM), BF16)
    w_sp = [jnp.where(causal, ws_ref[hd], 0.0).astype(BF16) for hd in range(HEADS)]
    b_sp = [jnp.broadcast_to(bs_ref[:, hd:hd + 1], (CHUNK, HEAD_DIM)) for hd in range(HEADS)]
    chunks = range(n_c)
    heads = range(HEADS)
    pairs = [(c, hd) for c in chunks for hd in heads]
    rsl = [slice(c * CHUNK, (c + 1) * CHUNK) for c in chunks]
    cols = [slice(hd * HEAD_DIM, (hd + 1) * HEAD_DIM) for hd in heads]
    state = [dict() for _ in range(n_seq)]

    def projections(t):
        d = state[t]

        def proj(w_ref, c0):
            return _dot(d["a"], w_ref[:, c0:c0 + piece])

        def norm_and_gates():
            d["a"] = _rms(h_ref[t], g2_ref[...]).astype(BF16)
            d["gcol"] = _dot(d["a"], wg_ref[...])

        def qk_piece(i):
            def run():
                qk_scr[t, SUBLANES:SUBLANES + rows, i * piece:(i + 1) * piece] = proj(wa_ref, i * piece)
            return run

        def gate_prefix_sums():
            i_rows, f_rows = [], []
            for c in chunks:
                x = d["gcol"][rsl[c], :].T[0:SUBLANES, :] + bias_ref[...]
                i_rows.append(x)
                f_rows.append(pltpu.roll(_log_sigmoid(x), HEADS, axis=0))
            d["log_i"] = jnp.concatenate(i_rows, axis=0)
            log_f = jnp.concatenate(f_rows, axis=0)
            pieces = jnp.concatenate(_split3(log_f), axis=0).astype(BF16)
            r = _dot(pieces, ut_ref[...])
            d["r"] = r[0:nr] + r[nr:2 * nr] + r[2 * nr:3 * nr]

        def conv_and_silu():
            conv = convb_ref[...]
            for k in range(CONV_WIDTH):
                x_k = qk_scr[t, SUBLANES - k:SUBLANES - k + rows, :]
                conv = conv + x_k * convw_ref[CONV_WIDTH - 1 - k:CONV_WIDTH - k, :]
            qk_scr[t, 0:SUBLANES, :] = qk_scr[t, rows:rows + SUBLANES, :]
            qk_act = conv * _sigmoid(conv)
            q_scr[t] = qk_act[:, 0:mw].astype(BF16)
            kf_scr[t] = qk_act[:, mw:2 * mw] * np.float32(HEAD_DIM ** -0.5)

        def v_piece(i):
            def run():
                c0 = 2 * mw + i * piece
                v_scr[t, :, i * piece:(i + 1) * piece] = proj(wa_ref, c0).astype(BF16)
            return run

        def o_piece(i):
            def run():
                c0 = 3 * mw + i * piece
                so_scr[t, :, i * piece:(i + 1) * piece] = _sigmoid(proj(wa_ref, c0))
            return run

        def gate_stabilisers():
            b = d["r"][:, 0:CHUNK]
            a_all = d["r"][:, CHUNK:2 * CHUNK]
            cb = d["log_i"] - b
            lane = lax.broadcasted_iota(jnp.int32, (nr, CHUNK), 1)
            cmax = cb
            for sh in (1, 2, 4, 8, 16, 32, 64):
                cmax = jnp.where(lane >= sh, jnp.maximum(cmax, pltpu.roll(cmax, sh, axis=1)), cmax)
            cmax_last = jnp.broadcast_to(cmax[:, CHUNK - 1:CHUNK], (nr, CHUNK))
            m_loc = a_all + cmax_last
            d["cb"] = cb
            d["e_state"] = jnp.exp(cb - cmax_last)
            m = m_scr[t]
            m_ins, d["s_prev"], d["s_loc"] = [], [], []
            for c in chunks:
                sl = slice(SUBLANES * c, SUBLANES * (c + 1))
                m_ins.append(m)
                m_new = jnp.maximum(a_all[sl] + m, m_loc[sl])
                d["s_prev"].append(jnp.exp(a_all[sl] + m - m_new))
                d["s_loc"].append(jnp.exp(m_loc[sl] - m_new))
                m = m_new
            m_scr[t] = m
            inter = b + jnp.concatenate(m_ins, axis=0)
            m_t = jnp.maximum(inter, b + cmax)
            row_blocks = (*_split3(b - m_t), *_split2(jnp.exp(inter - m_t)),
                          *_split2(jnp.exp(-m_t)))
            pad = jnp.zeros((LANES - _P_BLOCKS * SUBLANES, CHUNK), F32)
            d["p_col"] = []
            for c in chunks:
                sl = slice(SUBLANES * c, SUBLANES * (c + 1))
                p_row = jnp.concatenate([blk[sl] for blk in row_blocks] + [pad], axis=0)
                d["p_col"].append(p_row.T.astype(BF16))

        def u_piece(i):
            def run():
                ug_scr[t, :, i * piece:(i + 1) * piece] = _gelu_exact(proj(wb_ref, i * piece))
            return run

        def vg_piece(i):
            def run():
                d.setdefault("vg", []).append(_gelu_exact(proj(wb_ref, GMLP_WIDTH + i * piece)))
            return run

        def layer_norm_v():
            vg = jnp.concatenate(d.pop("vg"), axis=1)
            mu = jnp.mean(vg, axis=-1, keepdims=True)
            var = jnp.mean(jnp.square(vg - mu), axis=-1, keepdims=True)
            vn_scr[t] = ((vg - mu) * lax.rsqrt(var + EPS) * lng_ref[...] + lnb_ref[...]).astype(BF16)

        n_qk, n_v = 2 * mw // piece, mw // piece
        return ([norm_and_gates, qk_piece(0), gate_prefix_sums]
                + [qk_piece(i) for i in range(1, n_qk)] + [conv_and_silu]
                + [v_piece(i) for i in range(n_v)] + [gate_stabilisers]
                + [o_piece(i) for i in range(n_v)]
                + [u_piece(i) for i in range(GMLP_WIDTH // piece)]
                + [vg_piece(i) for i in range(GMLP_WIDTH // piece)] + [layer_norm_v])

    def chunk_stages(t, fill):
        d = state[t]
        group = 2 * HEADS

        def staged(fn):
            out = {}
            for i, (c, hd) in enumerate(pairs):
                out[c, hd] = fn(c, hd)
                if (i + 1) % group == 0:
                    fill()
            return out

        bc = staged(lambda c, hd: _dot(d["p_col"][c], sel_ref[hd]))
        qh = {(c, hd): q_scr[t, rsl[c], cols[hd]] for c, hd in pairs}
        v_aug = {(c, hd): jnp.concatenate([v_scr[t, rsl[c], cols[hd]], ones_blk], axis=1)
                 for c, hd in pairs}
        k_t = {(c, hd): kf_scr[t, rsl[c], cols[hd]].T for c, hd in pairs}
        s = staged(lambda c, hd: _dot(qh[c, hd], k_t[c, hd].astype(BF16)))

        def gmlp(c, hd):
            sv = _dot(w_sp[hd], vn_scr[t, rsl[c], cols[hd]]) + b_sp[hd]
            mix_scr[t, rsl[c], mw + hd * HEAD_DIM:mw + (hd + 1) * HEAD_DIM] = (
                ug_scr[t, rsl[c], cols[hd]] * sv).astype(BF16)
        staged(gmlp)

        def intra(c, hd):
            cb_row = d["cb"][SUBLANES * c + hd:SUBLANES * c + hd + 1, :]
            e_intra = jnp.exp(jnp.where(causal, bc[c, hd][:, 0:LANES] + cb_row, neg_inf))
            qk = (s[c, hd] * e_intra).astype(BF16)
            return _dot(qk, v_aug[c, hd])
        pv = staged(intra)

        def local_state(c, hd):
            es_row = d["e_state"][SUBLANES * c + hd:SUBLANES * c + hd + 1, :]
            return _dot((k_t[c, hd] * es_row).astype(BF16), v_aug[c, hd])
        c_loc = staged(local_state)

        c_aug = [caug_scr[t, hd] for hd in heads]

        def inter_chunk(c, hd):
            qc = _dot(qh[c, hd], c_aug[hd].astype(BF16))
            s_prev = d["s_prev"][c][hd:hd + 1, :]
            s_loc = d["s_loc"][c][hd:hd + 1, :]
            c_aug[hd] = (jnp.concatenate([s_prev, s_prev], axis=1) * c_aug[hd]
                         + jnp.concatenate([s_loc, s_loc], axis=1) * c_loc[c, hd])
            return qc
        qc = staged(inter_chunk)
        for hd in heads:
            caug_scr[t, hd] = c_aug[hd]

        hh = {}

        def normaliser(c, hd):
            e_inter_b = bc[c, hd][:, LANES:2 * LANES]
            num = e_inter_b * qc[c, hd][:, 0:LANES] + pv[c, hd][:, 0:LANES]
            den = e_inter_b * qc[c, hd][:, LANES:2 * LANES] + pv[c, hd][:, LANES:2 * LANES]
            hh[c, hd] = num / jnp.maximum(jnp.abs(den), bc[c, hd][:, 2 * LANES:3 * LANES])
            sq = jnp.concatenate(_split2(hh[c, hd] * hh[c, hd]), axis=1).astype(BF16)
            return _dot(sq, ones_ref[...])
        ssum = staged(normaliser)
        for c, hd in pairs:
            hn = (hh[c, hd] * lax.rsqrt(ssum[c, hd] * np.float32(1.0 / HEAD_DIM) + EPS)
                  * mhg_ref[:, cols[hd]])
            mix_scr[t, rsl[c], cols[hd]] = (hn * so_scr[t, rsl[c], cols[hd]]).astype(BF16)

    def output_projection(t):
        d = state[t]

        def out_piece(i):
            def run():
                d.setdefault("mix", []).append(
                    _dot(mix_scr[t], wout_ref[:, i * piece:(i + 1) * piece].astype(BF16)))
            return run

        def residual():
            mix = jnp.concatenate(d.pop("mix"), axis=1)
            o_ref[t] = h_ref[t] + _rms(mix, g3_ref[...])

        return [out_piece(i) for i in range(D_MODEL // piece)] + [residual]

    for f in projections(0):
        f()
    for t in range(n_seq):
        filler = projections(t + 1) if t + 1 < n_seq else []
        if t > 0:
            filler = output_projection(t - 1) + filler
        fill, flush = _interleaver(filler)
        chunk_stages(t, fill)
        flush()
    for f in output_projection(n_seq - 1):
        f()


def _mixer(h, batch, seq, g2, g3, w_in, conv_w, conv_b, b_if, mh_norm_g,
           gmlp_ln_g, gmlp_ln_b, w_spatial, b_spatial, w_out):
    mw = MLSTM_WIDTH
    n_j = seq // MIX_ROWS
    w_a, w_b, w_g = _mixer_weights(w_in)
    bias = jnp.broadcast_to(b_if.reshape(2 * HEADS, 1), (2 * HEADS, CHUNK))
    ut, sel, ones = _mixer_constants()
    h4 = h.reshape(batch // MIX_SEQS, MIX_SEQS, seq, D_MODEL)
    tile_spec = pl.BlockSpec((None, MIX_SEQS, MIX_ROWS, D_MODEL), lambda b, j: (b, 0, j, 0))
    args = (h4, g2.reshape(1, D_MODEL), g3.reshape(1, D_MODEL), w_a, w_b, w_g, bias,
            conv_w, conv_b.reshape(1, 2 * mw), mh_norm_g.reshape(1, mw),
            gmlp_ln_g.reshape(1, GMLP_WIDTH), gmlp_ln_b.reshape(1, GMLP_WIDTH),
            w_spatial, b_spatial.T, w_out, ut, sel, ones)
    in_specs = [tile_spec] + [_const_spec(x.shape) for x in args[1:]]

    def per_seq(shape, dtype):
        return pltpu.VMEM((MIX_SEQS,) + shape, dtype)

    out = pl.pallas_call(
        _mixer_kernel,
        grid=(batch // MIX_SEQS, n_j),
        in_specs=in_specs,
        out_specs=tile_spec,
        out_shape=jax.ShapeDtypeStruct(h4.shape, F32),
        scratch_shapes=[
            per_seq((SUBLANES + MIX_ROWS, 2 * mw), F32),
            per_seq((HEADS, HEAD_DIM, 2 * LANES), F32),
            per_seq((SUBLANES, CHUNK), F32),
            per_seq((MIX_ROWS, mw), BF16),
            per_seq((MIX_ROWS, mw), F32),
            per_seq((MIX_ROWS, mw), BF16),
            per_seq((MIX_ROWS, mw), F32),
            per_seq((MIX_ROWS, GMLP_WIDTH), BF16),
            per_seq((MIX_ROWS, GMLP_WIDTH), F32),
            per_seq((MIX_ROWS, mw + GMLP_WIDTH), BF16),
        ],
        compiler_params=pltpu.CompilerParams(
            dimension_semantics=("arbitrary", "arbitrary"),
            vmem_limit_bytes=VMEM_LIMIT),
        name="mixer",
    )(*args)
    return out.reshape(batch * seq, D_MODEL)


def kernel(x, p, ffn1_gu, ffn1_down, ffn2_gu, ffn2_down, w_in, conv_w, conv_b, b_if,
           mh_norm_g, gmlp_ln_g, gmlp_ln_b, w_spatial, b_spatial, w_out, w_ple,
           w_ple_gate, norm_g):
    batch, seq, _ = x.shape
    depth = p.shape[0]
    h = x.reshape(batch * seq, D_MODEL)
    for i in range(depth):
        g = norm_g[i]
        h = _ffn(h, g[0], g[1], ffn1_gu[i], ffn1_down[i])
        h = _mixer(h, batch, seq, g[2], g[3], w_in[i], conv_w[i], conv_b[i], b_if[i],
                   mh_norm_g[i], gmlp_ln_g[i], gmlp_ln_b[i], w_spatial[i], b_spatial[i],
                   w_out[i])
        h = _ffn(h, g[4], g[5], ffn2_gu[i], ffn2_down[i],
                 ple=(p[i].reshape(batch * seq, D_PLE), g[6], g[7], w_ple_gate[i], w_ple[i]))
    return h.reshape(batch, seq, D_MODEL)
```

```python
import functools

import jax
import jax.numpy as jnp
import numpy as np
from jax import lax
from jax.experimental import pallas as pl
from jax.experimental.pallas import tpu as pltpu

D_MODEL = 1024
D_PLE = 256
D_FF = 2816
HEADS = 4
HEAD_DIM = 128
MLSTM_WIDTH = HEADS * HEAD_DIM
GMLP_WIDTH = HEADS * HEAD_DIM
CHUNK = 128
CONV_WIDTH = 4
EPS = 1e-6

LANES = 128
SUBLANES = 8

FFN_ROWS = 512
HEAD_SLAB = 128
TAIL_SLAB = 256
FFN_FC = 256
MIX_ROWS = 512
MIX_SEQS = 2
VMEM_LIMIT = 56 * 1024 * 1024

F32 = jnp.float32
BF16 = jnp.bfloat16


def _rms(x, g):
    return x * lax.rsqrt(jnp.mean(x * x, axis=-1, keepdims=True) + EPS) * g


def _sigmoid(x):
    return 1.0 / (1.0 + jnp.exp(-x))


def _gelu_exact(x):
    return 0.5 * x * (1.0 + lax.erf(x * np.float32(np.sqrt(0.5))))


def _log_sigmoid(x):
    return -(jnp.maximum(-x, 0.0) + jnp.log1p(jnp.exp(-jnp.abs(x))))


def _dot(a, b):
    return jnp.dot(a, b, preferred_element_type=F32)


def _const_spec(shape):
    nd = len(shape)
    return pl.BlockSpec(shape, lambda *_: (0,) * nd, pipeline_mode=pl.Buffered(1))


def _ffn_kernel(*refs, with_ple):
    if with_ple:
        (x_ref, gin_ref, gout_ref, gu_ref, dn_ref, p_ref, g6_ref, g7_ref, wg_ref, wp_ref,
         o_ref, a_scr, act_scr) = refs
    else:
        x_ref, gin_ref, gout_ref, gu_ref, dn_ref, o_ref, a_scr, act_scr = refs
    n_f = D_FF // FFN_FC
    rows = x_ref.shape[0]
    head_slabs = [slice(r, r + HEAD_SLAB) for r in range(0, rows, HEAD_SLAB)]
    slabs = [slice(r, r + TAIL_SLAB) for r in range(0, rows, TAIL_SLAB)]

    a_scr[...] = _rms(x_ref[...], gin_ref[...]).astype(BF16)
    for f in range(n_f):
        cs = slice(f * FFN_FC, (f + 1) * FFN_FC)
        gate_w = gu_ref[:, cs].astype(BF16)
        up_w = gu_ref[:, D_FF + f * FFN_FC:D_FF + (f + 1) * FFN_FC].astype(BF16)
        row_sets = head_slabs if f == 0 else [slice(0, rows)]
        for rs in row_sets:
            g = _dot(a_scr[rs, :], gate_w)
            u = _dot(a_scr[rs, :], up_w)
            act_scr[rs, cs] = ((g * _sigmoid(g)) * u).astype(BF16)
    dn_w = dn_ref[...].astype(BF16)
    ys = [_dot(act_scr[rs, :], dn_w) for rs in slabs]
    hs = [x_ref[rs, :] + 0.5 * _rms(y, gout_ref[...]) for rs, y in zip(slabs, ys)]
    if with_ple:
        embs = [_dot(p_ref[rs, :].astype(BF16), wp_ref[...]) for rs in slabs]
        gates = [_sigmoid(_dot(_rms(h, g6_ref[...]).astype(BF16), wg_ref[...])) for h in hs]
        hs = [h + _rms(gate * emb, g7_ref[...]) for h, gate, emb in zip(hs, gates, embs)]
    for rs, h in zip(slabs, hs):
        o_ref[rs, :] = h


def _ffn(h, g_in, g_out, w_gu, w_down, ple=None):
    m = h.shape[0]
    gu, dn = w_gu, w_down
    row_spec = pl.BlockSpec((FFN_ROWS, D_MODEL), lambda i: (i, 0))
    vec_spec = _const_spec((1, D_MODEL))
    args = [h, g_in.reshape(1, D_MODEL), g_out.reshape(1, D_MODEL), gu, dn]
    in_specs = [row_spec, vec_spec, vec_spec, _const_spec(gu.shape), _const_spec(dn.shape)]
    if ple is not None:
        p, g6, g7, w_gate, w_ple = ple
        args += [p, g6.reshape(1, D_MODEL), g7.reshape(1, D_MODEL),
                 w_gate.astype(BF16), w_ple.astype(BF16)]
        in_specs += [pl.BlockSpec((FFN_ROWS, D_PLE), lambda i: (i, 0)), vec_spec, vec_spec,
                     _const_spec(w_gate.shape), _const_spec(w_ple.shape)]
    return pl.pallas_call(
        functools.partial(_ffn_kernel, with_ple=ple is not None),
        grid=(m // FFN_ROWS,),
        in_specs=in_specs,
        out_specs=row_spec,
        out_shape=jax.ShapeDtypeStruct(h.shape, F32),
        scratch_shapes=[pltpu.VMEM((FFN_ROWS, D_MODEL), BF16),
                        pltpu.VMEM((FFN_ROWS, D_FF), BF16)],
        compiler_params=pltpu.CompilerParams(
            dimension_semantics=("arbitrary",), vmem_limit_bytes=VMEM_LIMIT),
        name="ffn_ple" if ple is not None else "ffn",
    )(*args)


_P_RT = (0, 1, 2)
_P_EINTER = (3, 4)
_P_ENM = (5, 6)
_P_BLOCKS = 7


def _split2(x):
    hi = x.astype(BF16).astype(F32)
    lo = (x - hi).astype(BF16).astype(F32)
    return hi, lo


def _split3(x):
    hi = x.astype(BF16).astype(F32)
    r1 = x - hi
    mid = r1.astype(BF16).astype(F32)
    lo = (r1 - mid).astype(BF16).astype(F32)
    return hi, mid, lo


def _mixer_constants():
    s_in = np.arange(CHUNK)[:, None]
    s_out = np.arange(CHUNK)[None, :]
    ut = np.concatenate([(s_in <= s_out).astype(np.float32),
                         np.ones((CHUNK, CHUNK), np.float32)], axis=1)
    sel = np.zeros((HEADS, LANES, 3 * LANES), np.float32)
    for hd in range(HEADS):
        for third, blocks in enumerate((_P_RT, _P_EINTER, _P_ENM)):
            for blk in blocks:
                sel[hd, blk * SUBLANES + hd, third * LANES:(third + 1) * LANES] = 1.0
    ones = np.ones((2 * LANES, LANES), np.float32)
    return jnp.asarray(ut, BF16), jnp.asarray(sel, BF16), jnp.asarray(ones, BF16)


def _interleaver(closures):
    pending = list(closures)

    def fill():
        if pending:
            pending.pop(0)()

    def flush():
        while pending:
            pending.pop(0)()

    return fill, flush


def _mixer_weight_kernel(wt_ref, wa_ref, wb_ref, wg_ref):
    piece = 2 * LANES
    gate0 = wa_ref.shape[1]
    n_gate = 2 * HEADS
    for c0 in range(0, gate0, piece):
        wa_ref[:, c0:c0 + piece] = wt_ref[c0:c0 + piece, :].T.astype(BF16)
    lane = lax.broadcasted_iota(jnp.int32, (D_MODEL, LANES), 1)
    wg_ref[...] = jnp.where(lane < n_gate, wt_ref[gate0:gate0 + LANES, :].T, 0.0).astype(BF16)
    for c0 in range(0, wb_ref.shape[1], piece):
        r0 = gate0 + n_gate + c0
        wb_ref[:, c0:c0 + piece] = wt_ref[r0:r0 + piece, :].T.astype(BF16)


def _mixer_weights(w_in):
    gate0 = 4 * MLSTM_WIDTH
    n_b = 2 * GMLP_WIDTH
    w_t = jnp.swapaxes(w_in, 0, 1)
    full = lambda shape: pl.BlockSpec(shape, lambda i: (0, 0))
    return pl.pallas_call(
        _mixer_weight_kernel,
        grid=(1,),
        in_specs=[full(w_t.shape)],
        out_specs=[full((D_MODEL, gate0)), full((D_MODEL, n_b)), full((D_MODEL, LANES))],
        out_shape=[jax.ShapeDtypeStruct((D_MODEL, gate0), BF16),
                   jax.ShapeDtypeStruct((D_MODEL, n_b), BF16),
                   jax.ShapeDtypeStruct((D_MODEL, LANES), BF16)],
        compiler_params=pltpu.CompilerParams(
            dimension_semantics=("arbitrary",), vmem_limit_bytes=VMEM_LIMIT),
        name="mixer_weights",
    )(w_t)


def _mixer_kernel(h_ref, g2_ref, g3_ref, wa_ref, wb_ref, wg_ref, bias_ref,
                  convw_ref, convb_ref, mhg_ref, lng_ref, lnb_ref, ws_ref, bs_ref,
                  wout_ref, ut_ref, sel_ref, ones_ref, o_ref,
                  qk_scr, caug_scr, m_scr,
                  q_scr, kf_scr, v_scr, so_scr, vn_scr, ug_scr, mix_scr):
    n_seq, rows = h_ref.shape[0], h_ref.shape[1]
    n_c = rows // CHUNK
    mw = MLSTM_WIDTH
    nr = SUBLANES * n_c
    piece = 2 * LANES

    @pl.when(pl.program_id(1) == 0)
    def _():
        qk_scr[:, 0:SUBLANES, :] = jnp.zeros((n_seq, SUBLANES, 2 * mw), F32)
        caug_scr[...] = jnp.zeros_like(caug_scr)
        m_scr[...] = jnp.zeros_like(m_scr)

    r_i = lax.broadcasted_iota(jnp.int32, (CHUNK, CHUNK), 0)
    c_i = lax.broadcasted_iota(jnp.int32, (CHUNK, CHUNK), 1)
    causal = c_i <= r_i
    neg_inf = np.float32(-np.inf)
    ones_blk = jnp.ones((CHUNK, HEAD_DIM), BF16)
    w_sp = [jnp.where(causal, ws_ref[hd], 0.0).astype(BF16) for hd in range(HEADS)]
    b_sp = [jnp.broadcast_to(bs_ref[:, hd:hd + 1], (CHUNK, HEAD_DIM)) for hd in range(HEADS)]
    chunks = range(n_c)
    heads = range(HEADS)
    pairs = [(c, hd) for c in chunks for hd in heads]
    rsl = [slice(c * CHUNK, (c + 1) * CHUNK) for c in chunks]
    cols = [slice(hd * HEAD_DIM, (hd + 1) * HEAD_DIM) for hd in heads]
    state = [dict() for _ in range(n_seq)]

    def projections(t):
        d = state[t]

        def proj(w_ref, c0):
            return _dot(d["a"], w_ref[:, c0:c0 + piece])

        def norm_and_gates():
            d["a"] = _rms(h_ref[t], g2_ref[...]).astype(BF16)
            d["gcol"] = _dot(d["a"], wg_ref[...])

        def qk_piece(i):
            def run():
                qk_scr[t, SUBLANES:SUBLANES + rows, i * piece:(i + 1) * piece] = proj(wa_ref, i * piece)
            return run

        def gate_prefix_sums():
            i_rows, f_rows = [], []
            for c in chunks:
                x = d["gcol"][rsl[c], :].T[0:SUBLANES, :] + bias_ref[...]
                i_rows.append(x)
                f_rows.append(pltpu.roll(_log_sigmoid(x), HEADS, axis=0))
            d["log_i"] = jnp.concatenate(i_rows, axis=0)
            log_f = jnp.concatenate(f_rows, axis=0)
            pieces = jnp.concatenate(_split3(log_f), axis=0).astype(BF16)
            r = _dot(pieces, ut_ref[...])
            d["r"] = r[0:nr] + r[nr:2 * nr] + r[2 * nr:3 * nr]

        def conv_and_silu():
            conv = convb_ref[...]
            for k in range(CONV_WIDTH):
                x_k = qk_scr[t, SUBLANES - k:SUBLANES - k + rows, :]
                conv = conv + x_k * convw_ref[CONV_WIDTH - 1 - k:CONV_WIDTH - k, :]
            qk_scr[t, 0:SUBLANES, :] = qk_scr[t, rows:rows + SUBLANES, :]
            qk_act = conv * _sigmoid(conv)
            q_scr[t] = qk_act[:, 0:mw].astype(BF16)
            kf_scr[t] = qk_act[:, mw:2 * mw] * np.float32(HEAD_DIM ** -0.5)

        def v_piece(i):
            def run():
                c0 = 2 * mw + i * piece
                v_scr[t, :, i * piece:(i + 1) * piece] = proj(wa_ref, c0).astype(BF16)
            return run

        def o_piece(i):
            def run():
                c0 = 3 * mw + i * piece
                so_scr[t, :, i * piece:(i + 1) * piece] = _sigmoid(proj(wa_ref, c0))
            return run

        def gate_stabilisers():
            b = d["r"][:, 0:CHUNK]
            a_all = d["r"][:, CHUNK:2 * CHUNK]
            cb = d["log_i"] - b
            lane = lax.broadcasted_iota(jnp.int32, (nr, CHUNK), 1)
            cmax = cb
            for sh in (1, 2, 4, 8, 16, 32, 64):
                cmax = jnp.where(lane >= sh, jnp.maximum(cmax, pltpu.roll(cmax, sh, axis=1)), cmax)
            cmax_last = jnp.broadcast_to(cmax[:, CHUNK - 1:CHUNK], (nr, CHUNK))
            m_loc = a_all + cmax_last
            d["cb"] = cb
            d["e_state"] = jnp.exp(cb - cmax_last)
            m = m_scr[t]
            m_ins, d["s_prev"], d["s_loc"] = [], [], []
            for c in chunks:
                sl = slice(SUBLANES * c, SUBLANES * (c + 1))
                m_ins.append(m)
                m_new = jnp.maximum(a_all[sl] + m, m_loc[sl])
                d["s_prev"].append(jnp.exp(a_all[sl] + m - m_new))
                d["s_loc"].append(jnp.exp(m_loc[sl] - m_new))
                m = m_new
            m_scr[t] = m
            inter = b + jnp.concatenate(m_ins, axis=0)
            m_t = jnp.maximum(inter, b + cmax)
            row_blocks = (*_split3(b - m_t), *_split2(jnp.exp(inter - m_t)),
                          *_split2(jnp.exp(-m_t)))
            pad = jnp.zeros((LANES - _P_BLOCKS * SUBLANES, CHUNK), F32)
            d["p_col"] = []
            for c in chunks:
                sl = slice(SUBLANES * c, SUBLANES * (c + 1))
                p_row = jnp.concatenate([blk[sl] for blk in row_blocks] + [pad], axis=0)
                d["p_col"].append(p_row.T.astype(BF16))

        def u_piece(i):
            def run():
                ug_scr[t, :, i * piece:(i + 1) * piece] = _gelu_exact(proj(wb_ref, i * piece))
            return run

        def vg_piece(i):
            def run():
                d.setdefault("vg", []).append(_gelu_exact(proj(wb_ref, GMLP_WIDTH + i * piece)))
            return run

        def layer_norm_v():
            vg = jnp.concatenate(d.pop("vg"), axis=1)
            mu = jnp.mean(vg, axis=-1, keepdims=True)
            var = jnp.mean(jnp.square(vg - mu), axis=-1, keepdims=True)
            vn_scr[t] = ((vg - mu) * lax.rsqrt(var + EPS) * lng_ref[...] + lnb_ref[...]).astype(BF16)

        n_qk, n_v = 2 * mw // piece, mw // piece
        return ([norm_and_gates, qk_piece(0), gate_prefix_sums]
                + [qk_piece(i) for i in range(1, n_qk)] + [conv_and_silu]
                + [v_piece(i) for i in range(n_v)] + [gate_stabilisers]
                + [o_piece(i) for i in range(n_v)]
                + [u_piece(i) for i in range(GMLP_WIDTH // piece)]
                + [vg_piece(i) for i in range(GMLP_WIDTH // piece)] + [layer_norm_v])

    def chunk_stages(t, fill):
        d = state[t]
        group = 2 * HEADS

        def staged(fn):
            out = {}
            for i, (c, hd) in enumerate(pairs):
                out[c, hd] = fn(c, hd)
                if (i + 1) % group == 0:
                    fill()
            return out

        bc = staged(lambda c, hd: _dot(d["p_col"][c], sel_ref[hd]))
        qh = {(c, hd): q_scr[t, rsl[c], cols[hd]] for c, hd in pairs}
        v_aug = {(c, hd): jnp.concatenate([v_scr[t, rsl[c], cols[hd]], ones_blk], axis=1)
                 for c, hd in pairs}
        k_t = {(c, hd): kf_scr[t, rsl[c], cols[hd]].T for c, hd in pairs}
        s = staged(lambda c, hd: _dot(qh[c, hd], k_t[c, hd].astype(BF16)))

        def gmlp(c, hd):
            sv = _dot(w_sp[hd], vn_scr[t, rsl[c], cols[hd]]) + b_sp[hd]
            mix_scr[t, rsl[c], mw + hd * HEAD_DIM:mw + (hd + 1) * HEAD_DIM] = (
                ug_scr[t, rsl[c], cols[hd]] * sv).astype(BF16)
        staged(gmlp)

        def intra(c, hd):
            cb_row = d["cb"][SUBLANES * c + hd:SUBLANES * c + hd + 1, :]
            e_intra = jnp.exp(jnp.where(causal, bc[c, hd][:, 0:LANES] + cb_row, neg_inf))
            qk = (s[c, hd] * e_intra).astype(BF16)
            return _dot(qk, v_aug[c, hd])
        pv = staged(intra)

        def local_state(c, hd):
            es_row = d["e_state"][SUBLANES * c + hd:SUBLANES * c + hd + 1, :]
            return _dot((k_t[c, hd] * es_row).astype(BF16), v_aug[c, hd])
        c_loc = staged(local_state)

        c_aug = [caug_scr[t, hd] for hd in heads]

        def inter_chunk(c, hd):
            qc = _dot(qh[c, hd], c_aug[hd].astype(BF16))
            s_prev = d["s_prev"][c][hd:hd + 1, :]
            s_loc = d["s_loc"][c][hd:hd + 1, :]
            c_aug[hd] = (jnp.concatenate([s_prev, s_prev], axis=1) * c_aug[hd]
                         + jnp.concatenate([s_loc, s_loc], axis=1) * c_loc[c, hd])
            return qc
        qc = staged(inter_chunk)
        for hd in heads:
            caug_scr[t, hd] = c_aug[hd]

        hh = {}

        def normaliser(c, hd):
            e_inter_b = bc[c, hd][:, LANES:2 * LANES]
            num = e_inter_b * qc[c, hd][:, 0:LANES] + pv[c, hd][:, 0:LANES]
            den = e_inter_b * qc[c, hd][:, LANES:2 * LANES] + pv[c, hd][:, LANES:2 * LANES]
            hh[c, hd] = num / jnp.maximum(jnp.abs(den), bc[c, hd][:, 2 * LANES:3 * LANES])
            sq = jnp.concatenate(_split2(hh[c, hd] * hh[c, hd]), axis=1).astype(BF16)
            return _dot(sq, ones_ref[...])
        ssum = staged(normaliser)
        for c, hd in pairs:
            hn = (hh[c, hd] * lax.rsqrt(ssum[c, hd] * np.float32(1.0 / HEAD_DIM) + EPS)
                  * mhg_ref[:, cols[hd]])
            mix_scr[t, rsl[c], cols[hd]] = (hn * so_scr[t, rsl[c], cols[hd]]).astype(BF16)

    def output_projection(t):
        d = state[t]

        def out_piece(i):
            def run():
                d.setdefault("mix", []).append(
                    _dot(mix_scr[t], wout_ref[:, i * piece:(i + 1) * piece].astype(BF16)))
            return run

        def residual():
            mix = jnp.concatenate(d.pop("mix"), axis=1)
            o_ref[t] = h_ref[t] + _rms(mix, g3_ref[...])

        return [out_piece(i) for i in range(D_MODEL // piece)] + [residual]

    for f in projections(0):
        f()
    for t in range(n_seq):
        filler = projections(t + 1) if t + 1 < n_seq else []
        if t > 0:
            filler = output_projection(t - 1) + filler
        fill, flush = _interleaver(filler)
        chunk_stages(t, fill)
        flush()
    for f in output_projection(n_seq - 1):
        f()


def _mixer(h, batch, seq, g2, g3, w_in, conv_w, conv_b, b_if, mh_norm_g,
           gmlp_ln_g, gmlp_ln_b, w_spatial, b_spatial, w_out):
    mw = MLSTM_WIDTH
    n_j = seq // MIX_ROWS
    w_a, w_b, w_g = _mixer_weights(w_in)
    bias = jnp.broadcast_to(b_if.reshape(2 * HEADS, 1), (2 * HEADS, CHUNK))
    ut, sel, ones = _mixer_constants()
    h4 = h.reshape(batch // MIX_SEQS, MIX_SEQS, seq, D_MODEL)
    tile_spec = pl.BlockSpec((None, MIX_SEQS, MIX_ROWS, D_MODEL), lambda b, j: (b, 0, j, 0))
    args = (h4, g2.reshape(1, D_MODEL), g3.reshape(1, D_MODEL), w_a, w_b, w_g, bias,
            conv_w, conv_b.reshape(1, 2 * mw), mh_norm_g.reshape(1, mw),
            gmlp_ln_g.reshape(1, GMLP_WIDTH), gmlp_ln_b.reshape(1, GMLP_WIDTH),
            w_spatial, b_spatial.T, w_out, ut, sel, ones)
    in_specs = [tile_spec] + [_const_spec(x.shape) for x in args[1:]]

    def per_seq(shape, dtype):
        return pltpu.VMEM((MIX_SEQS,) + shape, dtype)

    out = pl.pallas_call(
        _mixer_kernel,
        grid=(batch // MIX_SEQS, n_j),
        in_specs=in_specs,
        out_specs=tile_spec,
        out_shape=jax.ShapeDtypeStruct(h4.shape, F32),
        scratch_shapes=[
            per_seq((SUBLANES + MIX_ROWS, 2 * mw), F32),
            per_seq((HEADS, HEAD_DIM, 2 * LANES), F32),
            per_seq((SUBLANES, CHUNK), F32),
            per_seq((MIX_ROWS, mw), BF16),
            per_seq((MIX_ROWS, mw), F32),
            per_seq((MIX_ROWS, mw), BF16),
            per_seq((MIX_ROWS, mw), F32),
            per_seq((MIX_ROWS, GMLP_WIDTH), BF16),
            per_seq((MIX_ROWS, GMLP_WIDTH), F32),
            per_seq((MIX_ROWS, mw + GMLP_WIDTH), BF16),
        ],
        compiler_params=pltpu.CompilerParams(
            dimension_semantics=("arbitrary", "arbitrary"),
            vmem_limit_bytes=VMEM_LIMIT),
        name="mixer",
    )(*args)
    return out.reshape(batch * seq, D_MODEL)


def kernel(x, p, ffn1_gu, ffn1_down, ffn2_gu, ffn2_down, w_in, conv_w, conv_b, b_if,
           mh_norm_g, gmlp_ln_g, gmlp_ln_b, w_spatial, b_spatial, w_out, w_ple,
           w_ple_gate, norm_g):
    batch, seq, _ = x.shape
    depth = p.shape[0]
    h = x.reshape(batch * seq, D_MODEL)
    for i in range(depth):
        g = norm_g[i]
        h = _ffn(h, g[0], g[1], ffn1_gu[i], ffn1_down[i])
        h = _mixer(h, batch, seq, g[2], g[3], w_in[i], conv_w[i], conv_b[i], b_if[i],
                   mh_norm_g[i], gmlp_ln_g[i], gmlp_ln_b[i], w_spatial[i], b_spatial[i],
                   w_out[i])
        h = _ffn(h, g[4], g[5], ffn2_gu[i], ffn2_down[i],
                 ple=(p[i].reshape(batch * seq, D_PLE), g[6], g[7], w_ple_gate[i], w_ple[i]))
    return h.reshape(batch, seq, D_MODEL)
```

```python
import functools

import jax
import jax.numpy as jnp
import numpy as np
from jax import lax
from jax.experimental import pallas as pl
from jax.experimental.pallas import tpu as pltpu

D_MODEL = 1024
D_PLE = 256
D_FF = 2816
HEADS = 4
HEAD_DIM = 128
MLSTM_WIDTH = HEADS * HEAD_DIM
GMLP_WIDTH = HEADS * HEAD_DIM
CHUNK = 128
CONV_WIDTH = 4
EPS = 1e-6
FFN1_GAIN, MIX_GAIN, FFN2_GAIN = 0, 2, 4

LANES = 128
SUBLANES = 8

FFN_ROWS = 512
HEAD_SLAB = 128
TAIL_SLAB = 256
FFN_FC = 256
MIX_ROWS = 512
MIX_SEQS = 2
VMEM_LIMIT = 56 * 1024 * 1024

F32 = jnp.float32
BF16 = jnp.bfloat16


def _rms(x, g):
    return x * lax.rsqrt(jnp.mean(x * x, axis=-1, keepdims=True) + EPS) * g


def _sigmoid(x):
    return 1.0 / (1.0 + jnp.exp(-x))


def _gelu_exact(x):
    return 0.5 * x * (1.0 + lax.erf(x * np.float32(np.sqrt(0.5))))


def _log_sigmoid(x):
    return -(jnp.maximum(-x, 0.0) + jnp.log1p(jnp.exp(-jnp.abs(x))))


def _dot(a, b):
    return jnp.dot(a, b, preferred_element_type=F32)


def _const_spec(shape):
    nd = len(shape)
    return pl.BlockSpec(shape, lambda *_: (0,) * nd, pipeline_mode=pl.Buffered(1))


def _ffn_kernel(*refs, first_gain, with_ple):
    if with_ple:
        x_ref, gains_ref, gu_ref, dn_ref, p_ref, wg_ref, wp_ref, o_ref, a_scr, act_scr = refs
    else:
        x_ref, gains_ref, gu_ref, dn_ref, o_ref, a_scr, act_scr = refs

    def gain(k):
        return gains_ref[k:k + 1, :]

    n_f = D_FF // FFN_FC
    rows = x_ref.shape[0]
    head_slabs = [slice(r, r + HEAD_SLAB) for r in range(0, rows, HEAD_SLAB)]
    slabs = [slice(r, r + TAIL_SLAB) for r in range(0, rows, TAIL_SLAB)]

    a_scr[...] = _rms(x_ref[...], gain(first_gain)).astype(BF16)
    for f in range(n_f):
        cs = slice(f * FFN_FC, (f + 1) * FFN_FC)
        gate_w = gu_ref[:, cs].astype(BF16)
        up_w = gu_ref[:, D_FF + f * FFN_FC:D_FF + (f + 1) * FFN_FC].astype(BF16)
        row_sets = head_slabs if f == 0 else [slice(0, rows)]
        for rs in row_sets:
            g = _dot(a_scr[rs, :], gate_w)
            u = _dot(a_scr[rs, :], up_w)
            act_scr[rs, cs] = ((g * _sigmoid(g)) * u).astype(BF16)
    dn_w = dn_ref[...].astype(BF16)
    ys = [_dot(act_scr[rs, :], dn_w) for rs in slabs]
    hs = [x_ref[rs, :] + 0.5 * _rms(y, gain(first_gain + 1)) for rs, y in zip(slabs, ys)]
    if with_ple:
        embs = [_dot(p_ref[rs, :].astype(BF16), wp_ref[...]) for rs in slabs]
        n_gain = gains_ref.shape[0]
        gates = [_sigmoid(_dot(_rms(h, gain(n_gain - 2)).astype(BF16), wg_ref[...])) for h in hs]
        hs = [h + _rms(gate * emb, gain(n_gain - 1)) for h, gate, emb in zip(hs, gates, embs)]
    for rs, h in zip(slabs, hs):
        o_ref[rs, :] = h


def _ffn(h, gains, first_gain, w_gu, w_down, ple=None):
    m = h.shape[0]
    gu, dn = w_gu, w_down
    row_spec = pl.BlockSpec((FFN_ROWS, D_MODEL), lambda i: (i, 0))
    args = [h, gains, gu, dn]
    in_specs = [row_spec, _const_spec(gains.shape), _const_spec(gu.shape), _const_spec(dn.shape)]
    if ple is not None:
        p, w_gate, w_ple = ple
        args += [p,
                 w_gate.astype(BF16), w_ple.astype(BF16)]
        in_specs += [pl.BlockSpec((FFN_ROWS, D_PLE), lambda i: (i, 0)),
                     _const_spec(w_gate.shape), _const_spec(w_ple.shape)]
    return pl.pallas_call(
        functools.partial(_ffn_kernel, first_gain=first_gain, with_ple=ple is not None),
        grid=(m // FFN_ROWS,),
        in_specs=in_specs,
        out_specs=row_spec,
        out_shape=jax.ShapeDtypeStruct(h.shape, F32),
        scratch_shapes=[pltpu.VMEM((FFN_ROWS, D_MODEL), BF16),
                        pltpu.VMEM((FFN_ROWS, D_FF), BF16)],
        compiler_params=pltpu.CompilerParams(
            dimension_semantics=("arbitrary",), vmem_limit_bytes=VMEM_LIMIT),
        name="ffn_ple" if ple is not None else "ffn",
    )(*args)


_P_RT = (0, 1, 2)
_P_EINTER = (3, 4)
_P_ENM = (5, 6)
_P_BLOCKS = 7


def _split2(x):
    hi = x.astype(BF16).astype(F32)
    lo = (x - hi).astype(BF16).astype(F32)
    return hi, lo


def _split3(x):
    hi = x.astype(BF16).astype(F32)
    r1 = x - hi
    mid = r1.astype(BF16).astype(F32)
    lo = (r1 - mid).astype(BF16).astype(F32)
    return hi, mid, lo


def _mixer_constants():
    s_in = np.arange(CHUNK)[:, None]
    s_out = np.arange(CHUNK)[None, :]
    ut = np.concatenate([(s_in <= s_out).astype(np.float32),
                         np.ones((CHUNK, CHUNK), np.float32)], axis=1)
    sel = np.zeros((HEADS, LANES, 3 * LANES), np.float32)
    for hd in range(HEADS):
        for third, blocks in enumerate((_P_RT, _P_EINTER, _P_ENM)):
            for blk in blocks:
                sel[hd, blk * SUBLANES + hd, third * LANES:(third + 1) * LANES] = 1.0
    ones = np.ones((2 * LANES, LANES), np.float32)
    return jnp.asarray(ut, BF16), jnp.asarray(sel, BF16), jnp.asarray(ones, BF16)


def _interleaver(closures):
    pending = list(closures)

    def fill():
        if pending:
            pending.pop(0)()

    def flush():
        while pending:
            pending.pop(0)()

    return fill, flush


def _mixer_weight_kernel(wt_ref, wa_ref, wb_ref, wg_ref):
    piece = 2 * LANES
    gate0 = wa_ref.shape[1]
    n_gate = 2 * HEADS
    for c0 in range(0, gate0, piece):
        wa_ref[:, c0:c0 + piece] = wt_ref[c0:c0 + piece, :].T.astype(BF16)
    lane = lax.broadcasted_iota(jnp.int32, (D_MODEL, LANES), 1)
    wg_ref[...] = jnp.where(lane < n_gate, wt_ref[gate0:gate0 + LANES, :].T, 0.0).astype(BF16)
    for c0 in range(0, wb_ref.shape[1], piece):
        r0 = gate0 + n_gate + c0
        wb_ref[:, c0:c0 + piece] = wt_ref[r0:r0 + piece, :].T.astype(BF16)


def _mixer_weights(w_in):
    gate0 = 4 * MLSTM_WIDTH
    n_b = 2 * GMLP_WIDTH
    w_t = jnp.swapaxes(w_in, 0, 1)
    full = lambda shape: pl.BlockSpec(shape, lambda i: (0, 0))
    return pl.pallas_call(
        _mixer_weight_kernel,
        grid=(1,),
        in_specs=[full(w_t.shape)],
        out_specs=[full((D_MODEL, gate0)), full((D_MODEL, n_b)), full((D_MODEL, LANES))],
        out_shape=[jax.ShapeDtypeStruct((D_MODEL, gate0), BF16),
                   jax.ShapeDtypeStruct((D_MODEL, n_b), BF16),
                   jax.ShapeDtypeStruct((D_MODEL, LANES), BF16)],
        compiler_params=pltpu.CompilerParams(
            dimension_semantics=("arbitrary",), vmem_limit_bytes=VMEM_LIMIT),
        name="mixer_weights",
    )(w_t)


def _mixer_kernel(h_ref, gains_ref, wa_ref, wb_ref, wg_ref, bias_ref,
                  convw_ref, convb_ref, mhg_ref, lng_ref, lnb_ref, ws_ref, bs_ref,
                  wout_ref, ut_ref, sel_ref, ones_ref, o_ref,
                  qk_scr, caug_scr, m_scr,
                  q_scr, kf_scr, v_scr, so_scr, vn_scr, ug_scr, mix_scr):
    n_seq, rows = h_ref.shape[0], h_ref.shape[1]
    n_c = rows // CHUNK
    mw = MLSTM_WIDTH
    nr = SUBLANES * n_c
    piece = 2 * LANES

    @pl.when(pl.program_id(1) == 0)
    def _():
        qk_scr[:, 0:SUBLANES, :] = jnp.zeros((n_seq, SUBLANES, 2 * mw), F32)
        caug_scr[...] = jnp.zeros_like(caug_scr)
        m_scr[...] = jnp.zeros_like(m_scr)

    r_i = lax.broadcasted_iota(jnp.int32, (CHUNK, CHUNK), 0)
    c_i = lax.broadcasted_iota(jnp.int32, (CHUNK, CHUNK), 1)
    causal = c_i <= r_i
    neg_inf = np.float32(-np.inf)
    ones_blk = jnp.ones((CHUNK, HEAD_DIM), BF16)
    w_sp = [jnp.where(causal, ws_ref[hd], 0.0).astype(BF16) for hd in range(HEADS)]
    b_sp = [jnp.broadcast_to(bs_ref[:, hd:hd + 1], (CHUNK, HEAD_DIM)) for hd in range(HEADS)]
    chunks = range(n_c)
    heads = range(HEADS)
    pairs = [(c, hd) for c in chunks for hd in heads]
    rsl = [slice(c * CHUNK, (c + 1) * CHUNK) for c in chunks]
    cols = [slice(hd * HEAD_DIM, (hd + 1) * HEAD_DIM) for hd in heads]
    state = [dict() for _ in range(n_seq)]

    def projections(t):
        d = state[t]

        def proj(w_ref, c0):
            return _dot(d["a"], w_ref[:, c0:c0 + piece])

        def norm_and_gates():
            d["a"] = _rms(h_ref[t], gains_ref[MIX_GAIN:MIX_GAIN + 1, :]).astype(BF16)
            d["gcol"] = _dot(d["a"], wg_ref[...])

        def qk_piece(i):
            def run():
                qk_scr[t, SUBLANES:SUBLANES + rows, i * piece:(i + 1) * piece] = proj(wa_ref, i * piece)
            return run

        def gate_prefix_sums():
            i_rows, f_rows = [], []
            for c in chunks:
                x = d["gcol"][rsl[c], :].T[0:SUBLANES, :] + bias_ref[...]
                i_rows.append(x)
                f_rows.append(pltpu.roll(_log_sigmoid(x), HEADS, axis=0))
            d["log_i"] = jnp.concatenate(i_rows, axis=0)
            log_f = jnp.concatenate(f_rows, axis=0)
            pieces = jnp.concatenate(_split3(log_f), axis=0).astype(BF16)
            r = _dot(pieces, ut_ref[...])
            d["r"] = r[0:nr] + r[nr:2 * nr] + r[2 * nr:3 * nr]

        def conv_and_silu():
            conv = convb_ref[...]
            for k in range(CONV_WIDTH):
                x_k = qk_scr[t, SUBLANES - k:SUBLANES - k + rows, :]
                conv = conv + x_k * convw_ref[CONV_WIDTH - 1 - k:CONV_WIDTH - k, :]
            qk_scr[t, 0:SUBLANES, :] = qk_scr[t, rows:rows + SUBLANES, :]
            qk_act = conv * _sigmoid(conv)
            q_scr[t] = qk_act[:, 0:mw].astype(BF16)
            kf_scr[t] = qk_act[:, mw:2 * mw] * np.float32(HEAD_DIM ** -0.5)

        def v_piece(i):
            def run():
                c0 = 2 * mw + i * piece
                v_scr[t, :, i * piece:(i + 1) * piece] = proj(wa_ref, c0).astype(BF16)
            return run

        def o_piece(i):
            def run():
                c0 = 3 * mw + i * piece
                so_scr[t, :, i * piece:(i + 1) * piece] = _sigmoid(proj(wa_ref, c0))
            return run

        def gate_stabilisers():
            b = d["r"][:, 0:CHUNK]
            a_all = d["r"][:, CHUNK:2 * CHUNK]
            cb = d["log_i"] - b
            lane = lax.broadcasted_iota(jnp.int32, (nr, CHUNK), 1)
            cmax = cb
            for sh in (1, 2, 4, 8, 16, 32, 64):
                cmax = jnp.where(lane >= sh, jnp.maximum(cmax, pltpu.roll(cmax, sh, axis=1)), cmax)
            cmax_last = jnp.broadcast_to(cmax[:, CHUNK - 1:CHUNK], (nr, CHUNK))
            m_loc = a_all + cmax_last
            d["cb"] = cb
            d["e_state"] = jnp.exp(cb - cmax_last)
            m = m_scr[t]
            m_ins, d["s_prev"], d["s_loc"] = [], [], []
            for c in chunks:
                sl = slice(SUBLANES * c, SUBLANES * (c + 1))
                m_ins.append(m)
                m_new = jnp.maximum(a_all[sl] + m, m_loc[sl])
                d["s_prev"].append(jnp.exp(a_all[sl] + m - m_new))
                d["s_loc"].append(jnp.exp(m_loc[sl] - m_new))
                m = m_new
            m_scr[t] = m
            inter = b + jnp.concatenate(m_ins, axis=0)
            m_t = jnp.maximum(inter, b + cmax)
            row_blocks = (*_split3(b - m_t), *_split2(jnp.exp(inter - m_t)),
                          *_split2(jnp.exp(-m_t)))
            pad = jnp.zeros((LANES - _P_BLOCKS * SUBLANES, CHUNK), F32)
            d["p_col"] = []
            for c in chunks:
                sl = slice(SUBLANES * c, SUBLANES * (c + 1))
                p_row = jnp.concatenate([blk[sl] for blk in row_blocks] + [pad], axis=0)
                d["p_col"].append(p_row.T.astype(BF16))

        def u_piece(i):
            def run():
                ug_scr[t, :, i * piece:(i + 1) * piece] = _gelu_exact(proj(wb_ref, i * piece))
            return run

        def vg_piece(i):
            def run():
                d.setdefault("vg", []).append(_gelu_exact(proj(wb_ref, GMLP_WIDTH + i * piece)))
            return run

        def layer_norm_v():
            vg = jnp.concatenate(d.pop("vg"), axis=1)
            mu = jnp.mean(vg, axis=-1, keepdims=True)
            var = jnp.mean(jnp.square(vg - mu), axis=-1, keepdims=True)
            vn_scr[t] = ((vg - mu) * lax.rsqrt(var + EPS) * lng_ref[...] + lnb_ref[...]).astype(BF16)

        n_qk, n_v = 2 * mw // piece, mw // piece
        return ([norm_and_gates, qk_piece(0), gate_prefix_sums]
                + [qk_piece(i) for i in range(1, n_qk)] + [conv_and_silu]
                + [v_piece(i) for i in range(n_v)] + [gate_stabilisers]
                + [o_piece(i) for i in range(n_v)]
                + [u_piece(i) for i in range(GMLP_WIDTH // piece)]
                + [vg_piece(i) for i in range(GMLP_WIDTH // piece)] + [layer_norm_v])

    def chunk_stages(t, fill):
        d = state[t]
        group = 2 * HEADS

        def staged(fn):
            out = {}
            for i, (c, hd) in enumerate(pairs):
                out[c, hd] = fn(c, hd)
                if (i + 1) % group == 0:
                    fill()
            return out

        bc = staged(lambda c, hd: _dot(d["p_col"][c], sel_ref[hd]))
        qh = {(c, hd): q_scr[t, rsl[c], cols[hd]] for c, hd in pairs}
        v_aug = {(c, hd): jnp.concatenate([v_scr[t, rsl[c], cols[hd]], ones_blk], axis=1)
                 for c, hd in pairs}
        k_t = {(c, hd): kf_scr[t, rsl[c], cols[hd]].T for c, hd in pairs}
        s = staged(lambda c, hd: _dot(qh[c, hd], k_t[c, hd].astype(BF16)))

        def gmlp(c, hd):
            sv = _dot(w_sp[hd], vn_scr[t, rsl[c], cols[hd]]) + b_sp[hd]
            mix_scr[t, rsl[c], mw + hd * HEAD_DIM:mw + (hd + 1) * HEAD_DIM] = (
                ug_scr[t, rsl[c], cols[hd]] * sv).astype(BF16)
        staged(gmlp)

        def intra(c, hd):
            cb_row = d["cb"][SUBLANES * c + hd:SUBLANES * c + hd + 1, :]
            e_intra = jnp.exp(jnp.where(causal, bc[c, hd][:, 0:LANES] + cb_row, neg_inf))
            qk = (s[c, hd] * e_intra).astype(BF16)
            return _dot(qk, v_aug[c, hd])
        pv = staged(intra)

        def local_state(c, hd):
            es_row = d["e_state"][SUBLANES * c + hd:SUBLANES * c + hd + 1, :]
            return _dot((k_t[c, hd] * es_row).astype(BF16), v_aug[c, hd])
        c_loc = staged(local_state)

        c_aug = [caug_scr[t, hd] for hd in heads]

        def inter_chunk(c, hd):
            qc = _dot(qh[c, hd], c_aug[hd].astype(BF16))
            s_prev = d["s_prev"][c][hd:hd + 1, :]
            s_loc = d["s_loc"][c][hd:hd + 1, :]
            c_aug[hd] = (jnp.concatenate([s_prev, s_prev], axis=1) * c_aug[hd]
                         + jnp.concatenate([s_loc, s_loc], axis=1) * c_loc[c, hd])
            return qc
        qc = staged(inter_chunk)
        for hd in heads:
            caug_scr[t, hd] = c_aug[hd]

        hh = {}

        def normaliser(c, hd):
            e_inter_b = bc[c, hd][:, LANES:2 * LANES]
            num = e_inter_b * qc[c, hd][:, 0:LANES] + pv[c, hd][:, 0:LANES]
            den = e_inter_b * qc[c, hd][:, LANES:2 * LANES] + pv[c, hd][:, LANES:2 * LANES]
            hh[c, hd] = num / jnp.maximum(jnp.abs(den), bc[c, hd][:, 2 * LANES:3 * LANES])
            sq = jnp.concatenate(_split2(hh[c, hd] * hh[c, hd]), axis=1).astype(BF16)
            return _dot(sq, ones_ref[...])
        ssum = staged(normaliser)
        for c, hd in pairs:
            hn = (hh[c, hd] * lax.rsqrt(ssum[c, hd] * np.float32(1.0 / HEAD_DIM) + EPS)
                  * mhg_ref[:, cols[hd]])
            mix_scr[t, rsl[c], cols[hd]] = (hn * so_scr[t, rsl[c], cols[hd]]).astype(BF16)

    def output_projection(t):
        d = state[t]

        def out_piece(i):
            def run():
                d.setdefault("mix", []).append(
                    _dot(mix_scr[t], wout_ref[:, i * piece:(i + 1) * piece].astype(BF16)))
            return run

        def residual():
            mix = jnp.concatenate(d.pop("mix"), axis=1)
            o_ref[t] = h_ref[t] + _rms(mix, gains_ref[MIX_GAIN + 1:MIX_GAIN + 2, :])

        return [out_piece(i) for i in range(D_MODEL // piece)] + [residual]

    for f in projections(0):
        f()
    for t in range(n_seq):
        filler = projections(t + 1) if t + 1 < n_seq else []
        if t > 0:
            filler = output_projection(t - 1) + filler
        fill, flush = _interleaver(filler)
        chunk_stages(t, fill)
        flush()
    for f in output_projection(n_seq - 1):
        f()


def _mixer(h, batch, seq, gains, w_in, conv_w, conv_b, b_if, mh_norm_g,
           gmlp_ln_g, gmlp_ln_b, w_spatial, b_spatial, w_out):
    mw = MLSTM_WIDTH
    n_j = seq // MIX_ROWS
    w_a, w_b, w_g = _mixer_weights(w_in)
    bias = jnp.broadcast_to(b_if.reshape(2 * HEADS, 1), (2 * HEADS, CHUNK))
    ut, sel, ones = _mixer_constants()
    h4 = h.reshape(batch // MIX_SEQS, MIX_SEQS, seq, D_MODEL)
    tile_spec = pl.BlockSpec((None, MIX_SEQS, MIX_ROWS, D_MODEL), lambda b, j: (b, 0, j, 0))
    args = (h4, gains, w_a, w_b, w_g, bias,
            conv_w, conv_b.reshape(1, 2 * mw), mh_norm_g.reshape(1, mw),
            gmlp_ln_g.reshape(1, GMLP_WIDTH), gmlp_ln_b.reshape(1, GMLP_WIDTH),
            w_spatial, b_spatial.T, w_out, ut, sel, ones)
    in_specs = [tile_spec] + [_const_spec(x.shape) for x in args[1:]]

    def per_seq(shape, dtype):
        return pltpu.VMEM((MIX_SEQS,) + shape, dtype)

    out = pl.pallas_call(
        _mixer_kernel,
        grid=(batch // MIX_SEQS, n_j),
        in_specs=in_specs,
        out_specs=tile_spec,
        out_shape=jax.ShapeDtypeStruct(h4.shape, F32),
        scratch_shapes=[
            per_seq((SUBLANES + MIX_ROWS, 2 * mw), F32),
            per_seq((HEADS, HEAD_DIM, 2 * LANES), F32),
            per_seq((SUBLANES, CHUNK), F32),
            per_seq((MIX_ROWS, mw), BF16),
            per_seq((MIX_ROWS, mw), F32),
            per_seq((MIX_ROWS, mw), BF16),
            per_seq((MIX_ROWS, mw), F32),
            per_seq((MIX_ROWS, GMLP_WIDTH), BF16),
            per_seq((MIX_ROWS, GMLP_WIDTH), F32),
            per_seq((MIX_ROWS, mw + GMLP_WIDTH), BF16),
        ],
        compiler_params=pltpu.CompilerParams(
            dimension_semantics=("arbitrary", "arbitrary"),
            vmem_limit_bytes=VMEM_LIMIT),
        name="mixer",
    )(*args)
    return out.reshape(batch * seq, D_MODEL)


def kernel(x, p, ffn1_gu, ffn1_down, ffn2_gu, ffn2_down, w_in, conv_w, conv_b, b_if,
           mh_norm_g, gmlp_ln_g, gmlp_ln_b, w_spatial, b_spatial, w_out, w_ple,
           w_ple_gate, norm_g):
    batch, seq, _ = x.shape
    depth = p.shape[0]
    h = x.reshape(batch * seq, D_MODEL)
    for i in range(depth):
        g = norm_g[i]
        h = _ffn(h, g, FFN1_GAIN, ffn1_gu[i], ffn1_down[i])
        h = _mixer(h, batch, seq, g, w_in[i], conv_w[i], conv_b[i], b_if[i],
                   mh_norm_g[i], gmlp_ln_g[i], gmlp_ln_b[i], w_spatial[i], b_spatial[i],
                   w_out[i])
        h = _ffn(h, g, FFN2_GAIN, ffn2_gu[i], ffn2_down[i],
                 ple=(p[i].reshape(batch * seq, D_PLE), w_ple_gate[i], w_ple[i]))
    return h.reshape(batch, seq, D_MODEL)
```

```python
import functools

import jax
import jax.numpy as jnp
import numpy as np
from jax import lax
from jax.experimental import pallas as pl
from jax.experimental.pallas import tpu as pltpu

D_MODEL = 1024
D_PLE = 256
D_FF = 2816
HEADS = 4
HEAD_DIM = 128
MLSTM_WIDTH = HEADS * HEAD_DIM
GMLP_WIDTH = HEADS * HEAD_DIM
CHUNK = 128
CONV_WIDTH = 4
EPS = 1e-6
FFN1_GAIN, MIX_GAIN, FFN2_GAIN = 0, 2, 4

LANES = 128
SUBLANES = 8

FFN_ROWS = 512
HEAD_SLAB = 128
TAIL_SLAB = 256
FFN_FC = 256
MIX_ROWS = 512
MIX_SEQS = 2
VMEM_LIMIT = 56 * 1024 * 1024

F32 = jnp.float32
BF16 = jnp.bfloat16


def _rms(x, g):
    return x * lax.rsqrt(jnp.mean(x * x, axis=-1, keepdims=True) + EPS) * g


def _sigmoid(x):
    return 1.0 / (1.0 + jnp.exp(-x))


def _gelu_exact(x):
    return 0.5 * x * (1.0 + lax.erf(x * np.float32(np.sqrt(0.5))))


def _log_sigmoid(x):
    return -(jnp.maximum(-x, 0.0) + jnp.log1p(jnp.exp(-jnp.abs(x))))


def _dot(a, b):
    return jnp.dot(a, b, preferred_element_type=F32)


def _const_spec(shape):
    nd = len(shape)
    return pl.BlockSpec(shape, lambda *_: (0,) * nd, pipeline_mode=pl.Buffered(1))


def _ffn_kernel(*refs, first_gain, with_ple):
    if with_ple:
        x_ref, gains_ref, gu_ref, dn_ref, p_ref, wg_ref, wp_ref, o_ref, a_scr, act_scr = refs
    else:
        x_ref, gains_ref, gu_ref, dn_ref, o_ref, a_scr, act_scr = refs

    def gain(k):
        return gains_ref[k:k + 1, :]

    n_f = D_FF // FFN_FC
    rows = x_ref.shape[0]
    head_slabs = [slice(r, r + HEAD_SLAB) for r in range(0, rows, HEAD_SLAB)]
    slabs = [slice(r, r + TAIL_SLAB) for r in range(0, rows, TAIL_SLAB)]

    a_scr[...] = _rms(x_ref[...], gain(first_gain)).astype(BF16)
    for f in range(n_f):
        cs = slice(f * FFN_FC, (f + 1) * FFN_FC)
        gate_w = gu_ref[:, cs].astype(BF16)
        up_w = gu_ref[:, D_FF + f * FFN_FC:D_FF + (f + 1) * FFN_FC].astype(BF16)
        row_sets = head_slabs if f == 0 else [slice(0, rows)]
        for rs in row_sets:
            g = _dot(a_scr[rs, :], gate_w)
            u = _dot(a_scr[rs, :], up_w)
            act_scr[rs, cs] = ((g * _sigmoid(g)) * u).astype(BF16)
    dn_w = dn_ref[...].astype(BF16)
    ys = [_dot(act_scr[rs, :], dn_w) for rs in slabs]
    hs = [x_ref[rs, :] + 0.5 * _rms(y, gain(first_gain + 1)) for rs, y in zip(slabs, ys)]
    if with_ple:
        embs = [_dot(p_ref[rs, :].astype(BF16), wp_ref[...]) for rs in slabs]
        n_gain = gains_ref.shape[0]
        gates = [_sigmoid(_dot(_rms(h, gain(n_gain - 2)).astype(BF16), wg_ref[...])) for h in hs]
        hs = [h + _rms(gate * emb, gain(n_gain - 1)) for h, gate, emb in zip(hs, gates, embs)]
    for rs, h in zip(slabs, hs):
        o_ref[rs, :] = h


def _ffn(h, gains, first_gain, w_gu, w_down, ple=None):
    m = h.shape[0]
    gu, dn = w_gu, w_down
    row_spec = pl.BlockSpec((FFN_ROWS, D_MODEL), lambda i: (i, 0))
    args = [h, gains, gu, dn]
    in_specs = [row_spec, _const_spec(gains.shape), _const_spec(gu.shape), _const_spec(dn.shape)]
    if ple is not None:
        p, w_gate, w_ple = ple
        args += [p,
                 w_gate.astype(BF16), w_ple.astype(BF16)]
        in_specs += [pl.BlockSpec((FFN_ROWS, D_PLE), lambda i: (i, 0)),
                     _const_spec(w_gate.shape), _const_spec(w_ple.shape)]
    return pl.pallas_call(
        functools.partial(_ffn_kernel, first_gain=first_gain, with_ple=ple is not None),
        grid=(m // FFN_ROWS,),
        in_specs=in_specs,
        out_specs=row_spec,
        out_shape=jax.ShapeDtypeStruct(h.shape, F32),
        scratch_shapes=[pltpu.VMEM((FFN_ROWS, D_MODEL), BF16),
                        pltpu.VMEM((FFN_ROWS, D_FF), BF16)],
        compiler_params=pltpu.CompilerParams(
            dimension_semantics=("arbitrary",), vmem_limit_bytes=VMEM_LIMIT),
        name="ffn_ple" if ple is not None else "ffn",
    )(*args)


_P_RT = (0, 1, 2)
_P_EINTER = (3, 4)
_P_ENM = (5, 6)
_P_BLOCKS = 7


def _split2(x):
    hi = x.astype(BF16).astype(F32)
    lo = (x - hi).astype(BF16).astype(F32)
    return hi, lo


def _split3(x):
    hi = x.astype(BF16).astype(F32)
    r1 = x - hi
    mid = r1.astype(BF16).astype(F32)
    lo = (r1 - mid).astype(BF16).astype(F32)
    return hi, mid, lo


def _mixer_constants():
    s_in = np.arange(CHUNK)[:, None]
    s_out = np.arange(CHUNK)[None, :]
    ut = np.concatenate([(s_in <= s_out).astype(np.float32),
                         np.ones((CHUNK, CHUNK), np.float32)], axis=1)
    sel = np.zeros((HEADS, LANES, 3 * LANES), np.float32)
    for hd in range(HEADS):
        for third, blocks in enumerate((_P_RT, _P_EINTER, _P_ENM)):
            for blk in blocks:
                sel[hd, blk * SUBLANES + hd, third * LANES:(third + 1) * LANES] = 1.0
    ones = np.ones((2 * LANES, LANES), np.float32)
    return jnp.asarray(ut, BF16), jnp.asarray(sel, BF16), jnp.asarray(ones, BF16)


def _interleaver(closures):
    pending = list(closures)

    def fill():
        if pending:
            pending.pop(0)()

    def flush():
        while pending:
            pending.pop(0)()

    return fill, flush


def _mixer_weight_kernel(wt_ref, wa_ref, wb_ref, wg_ref):
    piece = 2 * LANES
    gate0 = wa_ref.shape[1]
    n_gate = 2 * HEADS
    for c0 in range(0, gate0, piece):
        wa_ref[:, c0:c0 + piece] = wt_ref[c0:c0 + piece, :].T.astype(BF16)
    lane = lax.broadcasted_iota(jnp.int32, (D_MODEL, LANES), 1)
    wg_ref[...] = jnp.where(lane < n_gate, wt_ref[gate0:gate0 + LANES, :].T, 0.0).astype(BF16)
    for c0 in range(0, wb_ref.shape[1], piece):
        r0 = gate0 + n_gate + c0
        wb_ref[:, c0:c0 + piece] = wt_ref[r0:r0 + piece, :].T.astype(BF16)


def _mixer_weights(w_in):
    gate0 = 4 * MLSTM_WIDTH
    n_b = 2 * GMLP_WIDTH
    w_t = jnp.swapaxes(w_in, 0, 1)
    full = lambda shape: pl.BlockSpec(shape, lambda i: (0, 0))
    return pl.pallas_call(
        _mixer_weight_kernel,
        grid=(1,),
        in_specs=[full(w_t.shape)],
        out_specs=[full((D_MODEL, gate0)), full((D_MODEL, n_b)), full((D_MODEL, LANES))],
        out_shape=[jax.ShapeDtypeStruct((D_MODEL, gate0), BF16),
                   jax.ShapeDtypeStruct((D_MODEL, n_b), BF16),
                   jax.ShapeDtypeStruct((D_MODEL, LANES), BF16)],
        compiler_params=pltpu.CompilerParams(
            dimension_semantics=("arbitrary",), vmem_limit_bytes=VMEM_LIMIT),
        name="mixer_weights",
    )(w_t)


def _mixer_kernel(h_ref, gains_ref, wa_ref, wb_ref, wg_ref, bias_ref,
                  convw_ref, convb_ref, mhg_ref, lng_ref, lnb_ref, ws_ref, bs_ref,
                  wout_ref, ut_ref, sel_ref, ones_ref, o_ref,
                  qk_scr, caug_scr, m_scr,
                  q_scr, kf_scr, v_scr, so_scr, vn_scr, ug_scr, mix_scr):
    n_seq, rows = h_ref.shape[0], h_ref.shape[1]
    n_c = rows // CHUNK
    mw = MLSTM_WIDTH
    nr = SUBLANES * n_c
    piece = 2 * LANES

    @pl.when(pl.program_id(1) == 0)
    def _():
        qk_scr[:, 0:SUBLANES, :] = jnp.zeros((n_seq, SUBLANES, 2 * mw), F32)
        caug_scr[...] = jnp.zeros_like(caug_scr)
        m_scr[...] = jnp.zeros_like(m_scr)

    r_i = lax.broadcasted_iota(jnp.int32, (CHUNK, CHUNK), 0)
    c_i = lax.broadcasted_iota(jnp.int32, (CHUNK, CHUNK), 1)
    causal = c_i <= r_i
    neg_inf = np.float32(-np.inf)
    ones_blk = jnp.ones((CHUNK, HEAD_DIM), BF16)
    w_sp = [jnp.where(causal, ws_ref[hd], 0.0).astype(BF16) for hd in range(HEADS)]
    b_sp = [jnp.broadcast_to(bs_ref[:, hd:hd + 1], (CHUNK, HEAD_DIM)) for hd in range(HEADS)]
    chunks = range(n_c)
    heads = range(HEADS)
    pairs = [(c, hd) for c in chunks for hd in heads]
    rsl = [slice(c * CHUNK, (c + 1) * CHUNK) for c in chunks]
    cols = [slice(hd * HEAD_DIM, (hd + 1) * HEAD_DIM) for hd in heads]
    state = [dict() for _ in range(n_seq)]

    def projections(t):
        d = state[t]

        def proj(w_ref, c0):
            return _dot(d["a"], w_ref[:, c0:c0 + piece])

        def norm_and_gates():
            d["a"] = _rms(h_ref[t], gains_ref[MIX_GAIN:MIX_GAIN + 1, :]).astype(BF16)
            d["gcol"] = _dot(d["a"], wg_ref[...])

        def qk_piece(i):
            def run():
                qk_scr[t, SUBLANES:SUBLANES + rows, i * piece:(i + 1) * piece] = proj(wa_ref, i * piece)
            return run

        def gate_prefix_sums():
            i_rows, f_rows = [], []
            for c in chunks:
                x = d["gcol"][rsl[c], :].T[0:SUBLANES, :] + bias_ref[...]
                i_rows.append(x)
                f_rows.append(pltpu.roll(_log_sigmoid(x), HEADS, axis=0))
            d["log_i"] = jnp.concatenate(i_rows, axis=0)
            log_f = jnp.concatenate(f_rows, axis=0)
            pieces = jnp.concatenate(_split3(log_f), axis=0).astype(BF16)
            r = _dot(pieces, ut_ref[...])
            d["r"] = r[0:nr] + r[nr:2 * nr] + r[2 * nr:3 * nr]

        def conv_and_silu():
            conv = convb_ref[...]
            for k in range(CONV_WIDTH):
                x_k = qk_scr[t, SUBLANES - k:SUBLANES - k + rows, :]
                conv = conv + x_k * convw_ref[CONV_WIDTH - 1 - k:CONV_WIDTH - k, :]
            qk_scr[t, 0:SUBLANES, :] = qk_scr[t, rows:rows + SUBLANES, :]
            qk_act = conv * _sigmoid(conv)
            q_scr[t] = qk_act[:, 0:mw].astype(BF16)
            kf_scr[t] = qk_act[:, mw:2 * mw] * np.float32(HEAD_DIM ** -0.5)

        def v_piece(i):
            def run():
                c0 = 2 * mw + i * piece
                v_scr[t, :, i * piece:(i + 1) * piece] = proj(wa_ref, c0).astype(BF16)
            return run

        def o_piece(i):
            def run():
                c0 = 3 * mw + i * piece
                so_scr[t, :, i * piece:(i + 1) * piece] = _sigmoid(proj(wa_ref, c0))
            return run

        def gate_stabilisers():
            b = d["r"][:, 0:CHUNK]
            a_all = d["r"][:, CHUNK:2 * CHUNK]
            cb = d["log_i"] - b
            lane = lax.broadcasted_iota(jnp.int32, (nr, CHUNK), 1)
            cmax = cb
            for sh in (1, 2, 4, 8, 16, 32, 64):
                cmax = jnp.where(lane >= sh, jnp.maximum(cmax, pltpu.roll(cmax, sh, axis=1)), cmax)
            cmax_last = jnp.broadcast_to(cmax[:, CHUNK - 1:CHUNK], (nr, CHUNK))
            m_loc = a_all + cmax_last
            d["cb"] = cb
            d["e_state"] = jnp.exp(cb - cmax_last)
            m = m_scr[t]
            m_ins, d["s_prev"], d["s_loc"] = [], [], []
            for c in chunks:
                sl = slice(SUBLANES * c, SUBLANES * (c + 1))
                m_ins.append(m)
                m_new = jnp.maximum(a_all[sl] + m, m_loc[sl])
                d["s_prev"].append(jnp.exp(a_all[sl] + m - m_new))
                d["s_loc"].append(jnp.exp(m_loc[sl] - m_new))
                m = m_new
            m_scr[t] = m
            inter = b + jnp.concatenate(m_ins, axis=0)
            m_t = jnp.maximum(inter, b + cmax)
            row_blocks = (*_split3(b - m_t), *_split2(jnp.exp(inter - m_t)),
                          *_split2(jnp.exp(-m_t)))
            pad = jnp.zeros((LANES - _P_BLOCKS * SUBLANES, CHUNK), F32)
            d["p_col"] = []
            for c in chunks:
                sl = slice(SUBLANES * c, SUBLANES * (c + 1))
                p_row = jnp.concatenate([blk[sl] for blk in row_blocks] + [pad], axis=0)
                d["p_col"].append(p_row.T.astype(BF16))

        def u_piece(i):
            def run():
                ug_scr[t, :, i * piece:(i + 1) * piece] = _gelu_exact(proj(wb_ref, i * piece))
            return run

        def vg_piece(i):
            def run():
                d.setdefault("vg", []).append(_gelu_exact(proj(wb_ref, GMLP_WIDTH + i * piece)))
            return run

        def layer_norm_v():
            vg = jnp.concatenate(d.pop("vg"), axis=1)
            mu = jnp.mean(vg, axis=-1, keepdims=True)
            var = jnp.mean(jnp.square(vg - mu), axis=-1, keepdims=True)
            vn_scr[t] = ((vg - mu) * lax.rsqrt(var + EPS) * lng_ref[...] + lnb_ref[...]).astype(BF16)

        n_qk, n_v = 2 * mw // piece, mw // piece
        return ([norm_and_gates, qk_piece(0), gate_prefix_sums]
                + [qk_piece(i) for i in range(1, n_qk)] + [conv_and_silu]
                + [v_piece(i) for i in range(n_v)] + [gate_stabilisers]
                + [o_piece(i) for i in range(n_v)]
                + [u_piece(i) for i in range(GMLP_WIDTH // piece)]
                + [vg_piece(i) for i in range(GMLP_WIDTH // piece)] + [layer_norm_v])

    def chunk_stages(t, fill):
        half = n_c // 2
        for c0 in (0, half):
            half_chunk_stages(t, fill, [(c, hd) for c in range(c0, c0 + half) for hd in heads])

    def half_chunk_stages(t, fill, pairs):
        d = state[t]
        group = HEADS

        def staged(fn):
            out = {}
            for i, (c, hd) in enumerate(pairs):
                out[c, hd] = fn(c, hd)
                if (i + 1) % group == 0:
                    fill()
            return out

        bc = staged(lambda c, hd: _dot(d["p_col"][c], sel_ref[hd]))
        qh = {(c, hd): q_scr[t, rsl[c], cols[hd]] for c, hd in pairs}
        v_aug = {(c, hd): jnp.concatenate([v_scr[t, rsl[c], cols[hd]], ones_blk], axis=1)
                 for c, hd in pairs}
        k_t = {(c, hd): kf_scr[t, rsl[c], cols[hd]].T for c, hd in pairs}
        s = staged(lambda c, hd: _dot(qh[c, hd], k_t[c, hd].astype(BF16)))

        def gmlp(c, hd):
            sv = _dot(w_sp[hd], vn_scr[t, rsl[c], cols[hd]]) + b_sp[hd]
            mix_scr[t, rsl[c], mw + hd * HEAD_DIM:mw + (hd + 1) * HEAD_DIM] = (
                ug_scr[t, rsl[c], cols[hd]] * sv).astype(BF16)
        staged(gmlp)

        def intra(c, hd):
            cb_row = d["cb"][SUBLANES * c + hd:SUBLANES * c + hd + 1, :]
            e_intra = jnp.exp(jnp.where(causal, bc[c, hd][:, 0:LANES] + cb_row, neg_inf))
            qk = (s[c, hd] * e_intra).astype(BF16)
            return _dot(qk, v_aug[c, hd])
        pv = staged(intra)

        def local_state(c, hd):
            es_row = d["e_state"][SUBLANES * c + hd:SUBLANES * c + hd + 1, :]
            return _dot((k_t[c, hd] * es_row).astype(BF16), v_aug[c, hd])
        c_loc = staged(local_state)

        c_aug = [caug_scr[t, hd] for hd in heads]

        def inter_chunk(c, hd):
            qc = _dot(qh[c, hd], c_aug[hd].astype(BF16))
            s_prev = d["s_prev"][c][hd:hd + 1, :]
            s_loc = d["s_loc"][c][hd:hd + 1, :]
            c_aug[hd] = (jnp.concatenate([s_prev, s_prev], axis=1) * c_aug[hd]
                         + jnp.concatenate([s_loc, s_loc], axis=1) * c_loc[c, hd])
            return qc
        qc = staged(inter_chunk)
        for hd in heads:
            caug_scr[t, hd] = c_aug[hd]

        hh = {}

        def normaliser(c, hd):
            e_inter_b = bc[c, hd][:, LANES:2 * LANES]
            num = e_inter_b * qc[c, hd][:, 0:LANES] + pv[c, hd][:, 0:LANES]
            den = e_inter_b * qc[c, hd][:, LANES:2 * LANES] + pv[c, hd][:, LANES:2 * LANES]
            hh[c, hd] = num / jnp.maximum(jnp.abs(den), bc[c, hd][:, 2 * LANES:3 * LANES])
            sq = jnp.concatenate(_split2(hh[c, hd] * hh[c, hd]), axis=1).astype(BF16)
            return _dot(sq, ones_ref[...])
        ssum = staged(normaliser)
        for c, hd in pairs:
            hn = (hh[c, hd] * lax.rsqrt(ssum[c, hd] * np.float32(1.0 / HEAD_DIM) + EPS)
                  * mhg_ref[:, cols[hd]])
            mix_scr[t, rsl[c], cols[hd]] = (hn * so_scr[t, rsl[c], cols[hd]]).astype(BF16)

    def output_projection(t):
        d = state[t]

        def out_piece(i):
            def run():
                d.setdefault("mix", []).append(
                    _dot(mix_scr[t], wout_ref[:, i * piece:(i + 1) * piece].astype(BF16)))
            return run

        def residual():
            mix = jnp.concatenate(d.pop("mix"), axis=1)
            o_ref[t] = h_ref[t] + _rms(mix, gains_ref[MIX_GAIN + 1:MIX_GAIN + 2, :])

        return [out_piece(i) for i in range(D_MODEL // piece)] + [residual]

    for f in projections(0):
        f()
    for t in range(n_seq):
        filler = projections(t + 1) if t + 1 < n_seq else []
        if t > 0:
            filler = output_projection(t - 1) + filler
        fill, flush = _interleaver(filler)
        chunk_stages(t, fill)
        flush()
    for f in output_projection(n_seq - 1):
        f()


def _mixer(h, batch, seq, gains, w_in, conv_w, conv_b, b_if, mh_norm_g,
           gmlp_ln_g, gmlp_ln_b, w_spatial, b_spatial, w_out):
    mw = MLSTM_WIDTH
    n_j = seq // MIX_ROWS
    w_a, w_b, w_g = _mixer_weights(w_in)
    bias = jnp.broadcast_to(b_if.reshape(2 * HEADS, 1), (2 * HEADS, CHUNK))
    ut, sel, ones = _mixer_constants()
    h4 = h.reshape(batch // MIX_SEQS, MIX_SEQS, seq, D_MODEL)
    tile_spec = pl.BlockSpec((None, MIX_SEQS, MIX_ROWS, D_MODEL), lambda b, j: (b, 0, j, 0))
    args = (h4, gains, w_a, w_b, w_g, bias,
            conv_w, conv_b.reshape(1, 2 * mw), mh_norm_g.reshape(1, mw),
            gmlp_ln_g.reshape(1, GMLP_WIDTH), gmlp_ln_b.reshape(1, GMLP_WIDTH),
            w_spatial, b_spatial.T, w_out, ut, sel, ones)
    in_specs = [tile_spec] + [_const_spec(x.shape) for x in args[1:]]

    def per_seq(shape, dtype):
        return pltpu.VMEM((MIX_SEQS,) + shape, dtype)

    out = pl.pallas_call(
        _mixer_kernel,
        grid=(batch // MIX_SEQS, n_j),
        in_specs=in_specs,
        out_specs=tile_spec,
        out_shape=jax.ShapeDtypeStruct(h4.shape, F32),
        scratch_shapes=[
            per_seq((SUBLANES + MIX_ROWS, 2 * mw), F32),
            per_seq((HEADS, HEAD_DIM, 2 * LANES), F32),
            per_seq((SUBLANES, CHUNK), F32),
            per_seq((MIX_ROWS, mw), BF16),
            per_seq((MIX_ROWS, mw), F32),
            per_seq((MIX_ROWS, mw), BF16),
            per_seq((MIX_ROWS, mw), F32),
            per_seq((MIX_ROWS, GMLP_WIDTH), BF16),
            per_seq((MIX_ROWS, GMLP_WIDTH), F32),
            per_seq((MIX_ROWS, mw + GMLP_WIDTH), BF16),
        ],
        compiler_params=pltpu.CompilerParams(
            dimension_semantics=("arbitrary", "arbitrary"),
            vmem_limit_bytes=VMEM_LIMIT),
        name="mixer",
    )(*args)
    return out.reshape(batch * seq, D_MODEL)


def kernel(x, p, ffn1_gu, ffn1_down, ffn2_gu, ffn2_down, w_in, conv_w, conv_b, b_if,
           mh_norm_g, gmlp_ln_g, gmlp_ln_b, w_spatial, b_spatial, w_out, w_ple,
           w_ple_gate, norm_g):
    batch, seq, _ = x.shape
    depth = p.shape[0]
    h = x.reshape(batch * seq, D_MODEL)
    for i in range(depth):
        g = norm_g[i]
        h = _ffn(h, g, FFN1_GAIN, ffn1_gu[i], ffn1_down[i])
        h = _mixer(h, batch, seq, g, w_in[i], conv_w[i], conv_b[i], b_if[i],
                   mh_norm_g[i], gmlp_ln_g[i], gmlp_ln_b[i], w_spatial[i], b_spatial[i],
                   w_out[i])
        h = _ffn(h, g, FFN2_GAIN, ffn2_gu[i], ffn2_down[i],
                 ple=(p[i].reshape(batch * seq, D_PLE), w_ple_gate[i], w_ple[i]))
    return h.reshape(batch, seq, D_MODEL)
```
